```python
import math
import functools
import jax
import jax.numpy as jnp
from jax import lax
import numpy as np

D_MODEL = 4096
BATCH = 4
SEQ = 2048
DEPTH = 1
DEC_BATCH = 128
DEC_SEQ = 4
PAST_LEN = 16384
PAGE_SIZE = 128

MLA_HEADS = 16
Q_LORA = 1024
KV_LORA = 512
NOPE_DIM = 128
ROPE_DIM = 64
MLA_V_DIM = 128
ROPE_THETA = 10000.0
MLA_SCALE = (NOPE_DIM + ROPE_DIM) ** -0.5
Q_BLOCK = 128
MLSTM_HEADS = 8
MLSTM_QK_DIM = 128
MLSTM_V_DIM = 256
MLSTM_CHUNK = 64
M_EMPTY = -1e30
D_MIX = MLA_HEADS * MLA_V_DIM + MLSTM_HEADS * MLSTM_V_DIM
PEER_HEADS = 8
PEER_N_KEYS = 128
PEER_N_EXPERTS = PEER_N_KEYS * PEER_N_KEYS
PEER_QUERY_DIM = 256
PEER_TOPK = 16
PEER_TOKEN_BLOCK = 64
PLE_DIM = 256
EPS = 1e-6

IN_SIZES = (Q_LORA, KV_LORA, ROPE_DIM,
            MLSTM_HEADS * MLSTM_QK_DIM, MLSTM_HEADS * MLSTM_QK_DIM,
            MLSTM_HEADS * MLSTM_V_DIM, MLSTM_HEADS * MLSTM_V_DIM,
            MLSTM_HEADS, MLSTM_HEADS)
N_IN = sum(IN_SIZES)

kernel_name = 'hymba_mla_mlstm_peer_decode_step'


def split_columns(z):
    parts = []
    start = 0
    for size in IN_SIZES:
        parts.append(z[..., start:start + size])
        start += size
    return parts


def rmsnorm(x, g):
    xf = x.astype(jnp.float32)
    y = xf * lax.rsqrt(jnp.mean(xf * xf, axis=-1, keepdims=True) + EPS)
    return (y * g.astype(jnp.float32)).astype(x.dtype)


def head_rmsnorm(h, g):
    B, T = h.shape[0], h.shape[1]
    y = h * lax.rsqrt(jnp.mean(h * h, axis=-1, keepdims=True) + EPS)
    return y.reshape(B, T, -1) * g.astype(jnp.float32)


def rope_angles(pos):
    inv = ROPE_THETA ** (-jnp.arange(0, ROPE_DIM, 2, dtype=jnp.float32) / ROPE_DIM)
    ang = pos.astype(jnp.float32)[:, None] * inv[None, :]
    return jnp.cos(ang), jnp.sin(ang)


def apply_rope(x, cos, sin):
    xf = x.astype(jnp.float32)
    half = ROPE_DIM // 2
    x1, x2 = xf[..., :half], xf[..., half:]
    return jnp.concatenate([x1 * cos - x2 * sin, x1 * sin + x2 * cos], axis=-1).astype(x.dtype)


def latent_attend(q_abs, q_rope, c_kv, k_rope, mask):
    s = jnp.einsum('...qhc,...kc->...hqk', q_abs, c_kv) + jnp.einsum('...qhr,...kr->...hqk', q_rope, k_rope)
    s = jnp.where(mask, s.astype(jnp.float32) * MLA_SCALE, -jnp.inf)
    p = jax.nn.softmax(s, axis=-1)
    return jnp.einsum('...hqk,...kc->...qhc', p.astype(c_kv.dtype), c_kv)


def mla_prompt(q_abs, q_rope, c_kv, k_rope):
    B, S, H, C = q_abs.shape
    qb = math.gcd(S, Q_BLOCK)
    nb = S // qb
    qa = q_abs.reshape(B, nb, qb, H, C).swapaxes(0, 1)
    qr = q_rope.reshape(B, nb, qb, H, ROPE_DIM).swapaxes(0, 1)
    k_pos = jnp.arange(S, dtype=jnp.int32)

    def block(args):
        i, qa_b, qr_b = args
        q_pos = i * qb + jnp.arange(qb, dtype=jnp.int32)
        mask = k_pos[None, :] <= q_pos[:, None]
        return latent_attend(qa_b, qr_b, c_kv, k_rope, mask)

    o = lax.map(block, (jnp.arange(nb, dtype=jnp.int32), qa, qr))
    return o.swapaxes(0, 1).reshape(B, S, H, C)


def mla_sample(q_abs, q_rope, c_new, kr_new, cache_kv, cache_kr, page_table, layer):
    T = q_abs.shape[1]
    P = page_table.shape[1] * PAGE_SIZE
    k_pos = jnp.arange(P + T, dtype=jnp.int32)
    q_pos = P + jnp.arange(T, dtype=jnp.int32)
    mask = k_pos[None, :] <= q_pos[:, None]

    def one(args):
        qa, qr, pt, cn, kn = args
        c = jnp.concatenate([cache_kv[layer, pt].reshape(P, KV_LORA), cn], axis=0)
        kr = jnp.concatenate([cache_kr[layer, pt].reshape(P, ROPE_DIM), kn], axis=0)
        return latent_attend(qa, qr, c, kr, mask)

    return lax.map(one, (q_abs, q_rope, page_table, c_new, kr_new))


def mlstm_chunkwise(q, k, v, i_pre, f_pre, C0, n0, m0):
    f32 = jnp.float32
    B, T, H, DK = q.shape
    DV = v.shape[-1]
    L = math.gcd(T, MLSTM_CHUNK)
    NC = T // L
    k = k * (DK ** -0.5)

    def to_chunks(a):
        return a.astype(f32).reshape((B, NC, L) + a.shape[2:]).swapaxes(0, 1)

    causal = jnp.tril(jnp.ones((L, L), dtype=bool))

    def step(carry, xs):
        C, n, m = carry
        qc, kc, vc, ic, fc = xs
        b = jnp.cumsum(jax.nn.log_sigmoid(fc), axis=1).swapaxes(1, 2)
        ig = ic.swapaxes(1, 2)
        dmat = jnp.where(causal, b[..., :, None] - b[..., None, :] + ig[..., None, :], -jnp.inf)
        m_inter = b + m[..., None]
        m_t = jnp.maximum(m_inter, jnp.max(dmat, axis=-1))
        w_inter = jnp.exp(m_inter - m_t)
        s = jnp.einsum('blhd,bshd->bhls', qc, kc) * jnp.exp(dmat - m_t[..., None])
        num = w_inter[..., None] * jnp.einsum('blhd,bhde->bhle', qc, C) + jnp.einsum('bhls,bshe->bhle', s, vc)
        den = w_inter * jnp.einsum('blhd,bhd->bhl', qc, n) + jnp.sum(s, axis=-1)
        h = num / jnp.maximum(jnp.abs(den), jnp.exp(-m_t))[..., None]
        b_last = b[..., -1]
        w_end = b_last[..., None] - b + ig
        m_new = jnp.maximum(b_last + m, jnp.max(w_end, axis=-1))
        a_state = jnp.exp(b_last + m - m_new)
        a_tok = jnp.exp(w_end - m_new[..., None])
        C_new = a_state[..., None, None] * C + jnp.einsum('bhs,bshd,bshe->bhde', a_tok, kc, vc)
        n_new = a_state[..., None] * n + jnp.einsum('bhs,bshd->bhd', a_tok, kc)
        return (C_new, n_new, m_new), h.swapaxes(1, 2)

    xs = (to_chunks(q), to_chunks(k), to_chunks(v), to_chunks(i_pre), to_chunks(f_pre))
    (C, n, m), hs = lax.scan(step, (C0.astype(f32), n0.astype(f32), m0.astype(f32)), xs)
    h = hs.swapaxes(0, 1).reshape(B, T, H, DV)
    return h, C, n, m


def peer(x, w_q, sub_keys, w_down, w_up):
    N, D = x.shape
    q = (x @ w_q).reshape(N, PEER_HEADS, 2, PEER_QUERY_DIM // 2)
    s = jnp.einsum('nhpd,hpkd->nhpk', q, sub_keys).astype(jnp.float32)
    v1, i1 = lax.top_k(s[:, :, 0], PEER_TOPK)
    v2, i2 = lax.top_k(s[:, :, 1], PEER_TOPK)
    cand = (v1[..., :, None] + v2[..., None, :]).reshape(N, PEER_HEADS, PEER_TOPK * PEER_TOPK)
    cidx = (i1[..., :, None] * PEER_N_KEYS + i2[..., None, :]).reshape(N, PEER_HEADS, PEER_TOPK * PEER_TOPK)
    top_s, pos = lax.top_k(cand, PEER_TOPK)
    idx = jnp.take_along_axis(cidx, pos, axis=-1)
    g = jax.nn.softmax(top_s, axis=-1)
    tb = math.gcd(N, PEER_TOKEN_BLOCK)
    nb = N // tb

    def block(args):
        xb, ib, gb = args
        a = jnp.einsum('td,thkd->thk', xb, w_down[ib]).astype(jnp.float32)
        coef = (gb * jax.nn.gelu(a, approximate=False)).astype(xb.dtype)
        return jnp.einsum('thk,thkd->td', coef, w_up[ib])

    out = lax.map(block, (x.reshape(nb, tb, D), idx.reshape(nb, tb, PEER_HEADS, PEER_TOPK),
                          g.reshape(nb, tb, PEER_HEADS, PEER_TOPK)))
    return out.reshape(N, D)


def decoder_layer(x, p, cos, sin, attend, C0, n0, m0,
                  g_mix, w_in, g_q, w_uq, g_kv, w_uk, w_uv, b_i, b_f, g_mh, w_out,
                  g_ffn, pw_q, p_keys, pw_down, pw_up, g_ple, w_ple, w_ple_gate):
    B, T, D = x.shape
    h = rmsnorm(x, g_mix)
    cq, ckv_in, kr_in, mq, mk, mv, mo, mi, mf = split_columns(h @ w_in)
    q = (rmsnorm(cq, g_q) @ w_uq).reshape(B, T, MLA_HEADS, NOPE_DIM + ROPE_DIM)
    q_rope = apply_rope(q[..., NOPE_DIM:], cos[:, None, :], sin[:, None, :])
    c_kv = rmsnorm(ckv_in, g_kv)
    k_rope = apply_rope(kr_in, cos, sin)
    q_abs = jnp.einsum('bthd,chd->bthc', q[..., :NOPE_DIM], w_uk.reshape(KV_LORA, MLA_HEADS, NOPE_DIM))
    o_lat = attend(q_abs, q_rope, c_kv, k_rope)
    o_mla = jnp.einsum('bthc,chd->bthd', o_lat, w_uv.reshape(KV_LORA, MLA_HEADS, MLA_V_DIM)).reshape(B, T, -1)
    hm, C, n, m = mlstm_chunkwise(mq.reshape(B, T, MLSTM_HEADS, MLSTM_QK_DIM),
                                  mk.reshape(B, T, MLSTM_HEADS, MLSTM_QK_DIM),
                                  mv.reshape(B, T, MLSTM_HEADS, MLSTM_V_DIM),
                                  mi + b_i, mf + b_f, C0, n0, m0)
    o_ml = head_rmsnorm(hm, g_mh).astype(x.dtype) * jax.nn.sigmoid(mo)
    x = x + jnp.concatenate([o_mla, o_ml], axis=-1) @ w_out
    x = x + peer(rmsnorm(x, g_ffn).reshape(B * T, D), pw_q, p_keys, pw_down, pw_up).reshape(B, T, D)
    x = x + (p @ w_ple) * jax.nn.sigmoid(rmsnorm(x, g_ple) @ w_ple_gate)
    return x, c_kv, k_rope, C.astype(C0.dtype), n.astype(n0.dtype), m.astype(m0.dtype)


def setup_inputs(seed: int = 0) -> dict:
    key = jax.random.key(seed)
    ks = jax.random.split(key, 40)
    f32 = jnp.float32
    n_pages = PAST_LEN // PAGE_SIZE
    used = DEC_BATCH * n_pages
    n_phys = used + max(1, used // 4)

    def nrm(k, shape, scale):
        return jax.random.normal(k, shape, f32) * scale

    def gain(k, shape):
        return 1.0 + 0.01 * jax.random.normal(k, shape, f32)

    page_table = jax.random.permutation(ks[9], n_phys)[:used].reshape(DEC_BATCH, n_pages).astype(jnp.int32)
    return {
        'x_prompt': nrm(ks[0], (BATCH, SEQ, D_MODEL), 1.0),
        'x_sample': nrm(ks[1], (DEC_BATCH, DEC_SEQ, D_MODEL), 1.0),
        'p_prompt': nrm(ks[2], (DEPTH, BATCH, SEQ, PLE_DIM), 1.0),
        'p_sample': nrm(ks[3], (DEPTH, DEC_BATCH, DEC_SEQ, PLE_DIM), 1.0),
        'cache_kv_latent': nrm(ks[4], (DEPTH, n_phys, PAGE_SIZE, KV_LORA), 1.0),
        'cache_k_rope': nrm(ks[5], (DEPTH, n_phys, PAGE_SIZE, ROPE_DIM), 1.0),
        'state_mlstm_C': nrm(ks[6], (DEPTH, DEC_BATCH, MLSTM_HEADS, MLSTM_QK_DIM, MLSTM_V_DIM), 0.5),
        'state_mlstm_n': nrm(ks[7], (DEPTH, DEC_BATCH, MLSTM_HEADS, MLSTM_QK_DIM), 0.5),
        'state_mlstm_m': nrm(ks[8], (DEPTH, DEC_BATCH, MLSTM_HEADS), 1.0),
        'page_table': page_table,
        'g_mix_norm': gain(ks[10], (DEPTH, D_MODEL)),
        'w_in': nrm(ks[11], (DEPTH, D_MODEL, N_IN), D_MODEL ** -0.5),
        'g_q_latent': gain(ks[12], (DEPTH, Q_LORA)),
        'w_uq': nrm(ks[13], (DEPTH, Q_LORA, MLA_HEADS * (NOPE_DIM + ROPE_DIM)), Q_LORA ** -0.5),
        'g_kv_latent': gain(ks[14], (DEPTH, KV_LORA)),
        'w_uk': nrm(ks[15], (DEPTH, KV_LORA, MLA_HEADS * NOPE_DIM), KV_LORA ** -0.5),
        'w_uv': nrm(ks[16], (DEPTH, KV_LORA, MLA_HEADS * MLA_V_DIM), KV_LORA ** -0.5),
        'b_igate': nrm(ks[17], (DEPTH, MLSTM_HEADS), 0.1),
        'b_fgate': 3.0 + nrm(ks[18], (DEPTH, MLSTM_HEADS), 0.5),
        'g_mlstm_head': gain(ks[19], (DEPTH, MLSTM_HEADS * MLSTM_V_DIM)),
        'w_out': nrm(ks[20], (DEPTH, D_MIX, D_MODEL), D_MIX ** -0.5),
        'g_ffn_norm': gain(ks[21], (DEPTH, D_MODEL)),
        'peer_w_query': nrm(ks[22], (DEPTH, D_MODEL, PEER_HEADS * PEER_QUERY_DIM), D_MODEL ** -0.5),
        'peer_sub_keys': nrm(ks[23], (DEPTH, PEER_HEADS, 2, PEER_N_KEYS, PEER_QUERY_DIM // 2), (PEER_QUERY_DIM // 2) ** -0.5),
        'peer_w_down': nrm(ks[24], (DEPTH, PEER_N_EXPERTS, D_MODEL), D_MODEL ** -0.5),
        'peer_w_up': nrm(ks[25], (DEPTH, PEER_N_EXPERTS, D_MODEL), 0.5),
        'g_ple_norm': gain(ks[26], (DEPTH, D_MODEL)),
        'w_ple': nrm(ks[27], (DEPTH, PLE_DIM, D_MODEL), PLE_DIM ** -0.5),
        'w_ple_gate': nrm(ks[28], (DEPTH, D_MODEL, D_MODEL), D_MODEL ** -0.5),
        'g_final_norm': gain(ks[29], (D_MODEL,)),
    }


def reference(x_prompt, x_sample, p_prompt, p_sample, cache_kv_latent, cache_k_rope,
              state_mlstm_C, state_mlstm_n, state_mlstm_m, page_table,
              g_mix_norm, w_in, g_q_latent, w_uq, g_kv_latent, w_uk, w_uv, b_igate, b_fgate,
              g_mlstm_head, w_out, g_ffn_norm, peer_w_query, peer_sub_keys, peer_w_down, peer_w_up,
              g_ple_norm, w_ple, w_ple_gate, g_final_norm):
    B, S = x_prompt.shape[0], x_prompt.shape[1]
    T = x_sample.shape[1]
    n_past = page_table.shape[1] * PAGE_SIZE
    cos_p, sin_p = rope_angles(jnp.arange(S, dtype=jnp.int32))
    cos_s, sin_s = rope_angles(n_past + jnp.arange(T, dtype=jnp.int32))
    C_p0 = jnp.zeros((B, MLSTM_HEADS, MLSTM_QK_DIM, MLSTM_V_DIM), x_prompt.dtype)
    n_p0 = jnp.zeros((B, MLSTM_HEADS, MLSTM_QK_DIM), x_prompt.dtype)
    m_p0 = jnp.full((B, MLSTM_HEADS), M_EMPTY, jnp.float32)
    xp, xs = x_prompt, x_sample
    ckv_p, kr_p, ckv_s, kr_s = [], [], [], []
    Cp, np_, mp, Cs, ns, ms = [], [], [], [], [], []
    for l in range(DEPTH):
        lw = [a[l] for a in (g_mix_norm, w_in, g_q_latent, w_uq, g_kv_latent, w_uk, w_uv, b_igate, b_fgate,
                             g_mlstm_head, w_out, g_ffn_norm, peer_w_query, peer_sub_keys, peer_w_down, peer_w_up,
                             g_ple_norm, w_ple, w_ple_gate)]
        xp, c1, k1, C1, n1, m1 = decoder_layer(xp, p_prompt[l], cos_p, sin_p, mla_prompt, C_p0, n_p0, m_p0, *lw)
        attend_s = functools.partial(mla_sample, cache_kv=cache_kv_latent, cache_kr=cache_k_rope,
                                     page_table=page_table, layer=l)
        xs, c2, k2, C2, n2, m2 = decoder_layer(xs, p_sample[l], cos_s, sin_s, attend_s,
                                               state_mlstm_C[l], state_mlstm_n[l], state_mlstm_m[l], *lw)
        ckv_p.append(c1); kr_p.append(k1); ckv_s.append(c2); kr_s.append(k2)
        Cp.append(C1); np_.append(n1); mp.append(m1); Cs.append(C2); ns.append(n2); ms.append(m2)
    y_prompt = rmsnorm(xp, g_final_norm)
    y_sample = rmsnorm(xs, g_final_norm)
    kv_latent_prompt = jnp.stack(ckv_p)
    k_rope_prompt = jnp.stack(kr_p)
    kv_latent_sample = jnp.stack(ckv_s)
    k_rope_sample = jnp.stack(kr_s)
    C_prompt = jnp.stack(Cp)
    n_prompt = jnp.stack(np_)
    m_prompt = jnp.stack(mp)
    C_sample = jnp.stack(Cs)
    n_sample = jnp.stack(ns)
    m_sample = jnp.stack(ms)
    return (y_prompt, y_sample, kv_latent_prompt, k_rope_prompt, kv_latent_sample, k_rope_sample,
            C_prompt, n_prompt, m_prompt, C_sample, n_sample, m_sample)
```

```python
import functools
import math

import jax
import jax.numpy as jnp
from jax import lax
from jax.experimental import pallas as pl
from jax.experimental.pallas import tpu as pltpu

F32, BF16, I32 = jnp.float32, jnp.bfloat16, jnp.int32

MLA_HEADS = 16
NOPE_DIM = 128
ROPE_DIM = 64
MLA_V_DIM = 128
ROPE_THETA = 10000.0
MLSTM_HEADS = 8
MLSTM_QK_DIM = 128
MLSTM_V_DIM = 256
MLSTM_CHUNK = 64
M_EMPTY = -1e30
PEER_HEADS = 8
PEER_N_KEYS = 128
PEER_TOPK = 16
PAGE_SIZE = 128
EPS = 1e-6
INV_SQRT2 = 0.7071067811865476

LANE = 128
SUBLANE = 8
BF16_SUBLANE = 16
VMEM_LIMIT_BYTES = 60 * 1024 * 1024

KV_W = 640
NOT_A_RANK = 1.0e9


def _tile(n, cap, mult):
    best = None
    d = mult
    while d <= min(n, cap):
        if n % d == 0:
            best = d
        d += mult
    return n if best is None else best


def _cparams(sem):
    return pltpu.CompilerParams(dimension_semantics=sem, vmem_limit_bytes=VMEM_LIMIT_BYTES)


def _dot(a, b):
    return jnp.dot(a, b, preferred_element_type=F32)


def _dot_nt(a, b):
    return lax.dot_general(a, b, (((1,), (1,)), ((), ())), preferred_element_type=F32)


def _dot_tn(a, b):
    return lax.dot_general(a, b, (((0,), (0,)), ((), ())), preferred_element_type=F32)


def _rms(x, g):
    return x * lax.rsqrt(jnp.mean(x * x, axis=-1, keepdims=True) + EPS) * g


def _norm_kernel(x_ref, g_ref, o_ref):
    o_ref[...] = _rms(x_ref[...], g_ref[...]).astype(o_ref.dtype)


def _rmsnorm(x, g, out_dtype):
    n, d = x.shape
    tm = _tile(n, 256, BF16_SUBLANE)
    return pl.pallas_call(
        _norm_kernel,
        grid=(n // tm,),
        in_specs=[pl.BlockSpec((tm, d), lambda i: (i, 0)), pl.BlockSpec((1, d), lambda i: (0, 0))],
        out_specs=pl.BlockSpec((tm, d), lambda i: (i, 0)),
        out_shape=jax.ShapeDtypeStruct((n, d), out_dtype),
        compiler_params=_cparams(("parallel",)),
        name="rmsnorm",
    )(x, g.reshape(1, d))


def _add_norm_kernel(a_ref, b_ref, g_ref, s_ref, o_ref):
    s = a_ref[...] + b_ref[...]
    s_ref[...] = s
    o_ref[...] = _rms(s, g_ref[...]).astype(o_ref.dtype)


def _add_rmsnorm(a, b, g):
    n, d = a.shape
    tm = _tile(n, 128, BF16_SUBLANE)
    row = pl.BlockSpec((tm, d), lambda i: (i, 0))
    return pl.pallas_call(
        _add_norm_kernel,
        grid=(n // tm,),
        in_specs=[row, row, pl.BlockSpec((1, d), lambda i: (0, 0))],
        out_specs=[row, row],
        out_shape=[jax.ShapeDtypeStruct((n, d), F32), jax.ShapeDtypeStruct((n, d), BF16)],
        compiler_params=_cparams(("parallel",)),
        name="add_rmsnorm",
    )(a, b, g.reshape(1, d))


def _mm_kernel(a_ref, b_ref, o_ref):
    o_ref[...] = _dot(a_ref[...], b_ref[...]).astype(o_ref.dtype)


def _matmul(a, b, out_dtype=F32):
    m, k = a.shape
    n = b.shape[1]
    tm = _tile(m, 1088, BF16_SUBLANE)
    tn = _tile(n, 1024, LANE)
    return pl.pallas_call(
        _mm_kernel,
        grid=(m // tm, n // tn),
        in_specs=[pl.BlockSpec((tm, k), lambda i, j: (i, 0)), pl.BlockSpec((k, tn), lambda i, j: (0, j))],
        out_specs=pl.BlockSpec((tm, tn), lambda i, j: (i, j)),
        out_shape=jax.ShapeDtypeStruct((m, n), out_dtype),
        compiler_params=_cparams(("parallel", "parallel")),
        name="matmul",
    )(a, b)


def _outproj_kernel(a1_ref, a2_ref, b1_ref, b2_ref, r_ref, o_ref):
    o_ref[...] = r_ref[...] + (_dot(a1_ref[...], b1_ref[...]) + _dot(a2_ref[...], b2_ref[...]))


def _outproj(a1, a2, w, r):
    m, k1 = a1.shape
    k2 = a2.shape[1]
    n = w.shape[1]
    tm = _tile(m, 1088, BF16_SUBLANE)
    tn = _tile(n, 512, LANE)
    nb1 = k1 // k2
    return pl.pallas_call(
        _outproj_kernel,
        grid=(m // tm, n // tn),
        in_specs=[
            pl.BlockSpec((tm, k1), lambda i, j: (i, 0)),
            pl.BlockSpec((tm, k2), lambda i, j: (i, 0)),
            pl.BlockSpec((k1, tn), lambda i, j: (0, j)),
            pl.BlockSpec((k2, tn), lambda i, j: (nb1, j)),
            pl.BlockSpec((tm, tn), lambda i, j: (i, j)),
        ],
        out_specs=pl.BlockSpec((tm, tn), lambda i, j: (i, j)),
        out_shape=jax.ShapeDtypeStruct((m, n), F32),
        compiler_params=_cparams(("parallel", "parallel")),
        name="outproj",
    )(a1, a2, w, w, r)


def _ple_kernel(xg_ref, wg_ref, p_ref, wp_ref, x_ref, o_ref):
    gate = _dot(xg_ref[...], wg_ref[...])
    o_ref[...] = x_ref[...] + _dot(p_ref[...], wp_ref[...]) * jax.nn.sigmoid(gate)


def _ple(xg, w_gate, p, w_ple, x):
    m, k = xg.shape
    n = w_gate.shape[1]
    kp = p.shape[1]
    tm = _tile(m, 1088, BF16_SUBLANE)
    tn = _tile(n, 512, LANE)
    return pl.pallas_call(
        _ple_kernel,
        grid=(m // tm, n // tn),
        in_specs=[
            pl.BlockSpec((tm, k), lambda i, j: (i, 0)),
            pl.BlockSpec((k, tn), lambda i, j: (0, j)),
            pl.BlockSpec((tm, kp), lambda i, j: (i, 0)),
            pl.BlockSpec((kp, tn), lambda i, j: (0, j)),
            pl.BlockSpec((tm, tn), lambda i, j: (i, j)),
        ],
        out_specs=pl.BlockSpec((tm, tn), lambda i, j: (i, j)),
        out_shape=jax.ShapeDtypeStruct((m, n), F32),
        compiler_params=_cparams(("parallel", "parallel")),
        name="ple",
    )(xg, w_gate, p, w_ple, x)


def _qprep_kernel(cq_ref, ckv_ref, krk_ref, cs_ref, gq_ref, gkv_ref, wn_ref, wr_ref, wrr_ref, wuk_ref,
                  q_ref, ckv_out_ref, kr_out_ref, kv_ref):
    cqn = _rms(cq_ref[...], gq_ref[...]).astype(BF16)
    cs = cs_ref[...]
    cos1, sin1 = cs[:, :LANE], cs[:, LANE:]
    cos_t = jnp.concatenate([cos1] * MLA_HEADS, axis=1)
    sin_t = jnp.concatenate([sin1] * MLA_HEADS, axis=1)
    qn = _dot(cqn, wn_ref[...])
    qr = _dot(cqn, wr_ref[...]) * cos_t + _dot(cqn, wrr_ref[...]) * sin_t
    for h in range(MLA_HEADS):
        sl = slice(h * LANE, (h + 1) * LANE)
        qa = _dot(qn[:, sl].astype(BF16), wuk_ref[h])
        q_ref[0, h, :, 0:512] = qa.astype(BF16)
        q_ref[0, h, :, 512:KV_W] = qr[:, sl].astype(BF16)
    c = _rms(ckv_ref[...], gkv_ref[...])
    ckv_out_ref[...] = c
    krk = krk_ref[...]
    kr = krk[:, :LANE] * cos1 + krk[:, LANE:] * sin1
    kr_out_ref[...] = kr[:, :ROPE_DIM]
    kv_ref[:, 0:512] = c.astype(BF16)
    kv_ref[:, 512:KV_W] = kr.astype(BF16)


def _qprep(z, cs, g_q, g_kv, wn, wr, wrr, wuk, tm, zo):
    n = z.shape[0]
    q_lora = wn.shape[0]
    kv_lora = wuk.shape[2]
    const2 = lambda i: (0, 0)
    return pl.pallas_call(
        _qprep_kernel,
        grid=(n // tm,),
        in_specs=[
            pl.BlockSpec((tm, q_lora), lambda i: (i, zo["cq"] // q_lora)),
            pl.BlockSpec((tm, kv_lora), lambda i: (i, zo["ckv"] // kv_lora)),
            pl.BlockSpec((tm, 2 * LANE), lambda i: (i, zo["kr"] // (2 * LANE))),
            pl.BlockSpec((tm, 2 * LANE), lambda i: (i, 0)),
            pl.BlockSpec((1, q_lora), const2),
            pl.BlockSpec((1, kv_lora), const2),
            pl.BlockSpec(wn.shape, const2),
            pl.BlockSpec(wr.shape, const2),
            pl.BlockSpec(wrr.shape, const2),
            pl.BlockSpec(wuk.shape, lambda i: (0, 0, 0)),
        ],
        out_specs=[
            pl.BlockSpec((1, MLA_HEADS, tm, KV_W), lambda i: (i, 0, 0, 0)),
            pl.BlockSpec((tm, kv_lora), lambda i: (i, 0)),
            pl.BlockSpec((tm, ROPE_DIM), lambda i: (i, 0)),
            pl.BlockSpec((tm, KV_W), lambda i: (i, 0)),
        ],
        out_shape=[
            jax.ShapeDtypeStruct((n // tm, MLA_HEADS, tm, KV_W), BF16),
            jax.ShapeDtypeStruct((n, kv_lora), F32),
            jax.ShapeDtypeStruct((n, ROPE_DIM), F32),
            jax.ShapeDtypeStruct((n, KV_W), BF16),
        ],
        compiler_params=_cparams(("parallel",)),
        name="mla_qprep",
    )(z, z, z, cs, g_q.reshape(1, -1), g_kv.reshape(1, -1), wn, wr, wrr, wuk)


def _softmax_step(s, v, m_sc, l_sc, acc_sc):
    m_old = m_sc[...]
    m_new = jnp.maximum(m_old, jnp.max(s, axis=-1, keepdims=True))
    alpha = jnp.exp(m_old - m_new)
    p = jnp.exp(s - m_new)
    l_sc[...] = alpha * l_sc[...] + jnp.sum(p, axis=-1, keepdims=True)
    acc_sc[...] = alpha * acc_sc[...] + _dot(p.astype(BF16), v)
    m_sc[...] = m_new


def _attn_prompt_kernel(q_ref, kv_ref, wuv_ref, o_ref, m_sc, l_sc, acc_sc, *, tq, tk, scale, kv_lora):
    qi = pl.program_id(1)
    rows = MLA_HEADS * tq
    q = q_ref[0].reshape(rows, KV_W)
    m_sc[...] = jnp.full(m_sc.shape, -jnp.inf, F32)
    l_sc[...] = jnp.zeros(l_sc.shape, F32)
    acc_sc[...] = jnp.zeros(acc_sc.shape, F32)
    qpos = qi * tq + lax.rem(lax.broadcasted_iota(I32, (rows, 1), 0), tq)
    nkb = ((qi + 1) * tq + tk - 1) // tk

    def body(j, carry):
        k0 = pl.multiple_of(j * tk, tk)
        kblk = kv_ref[pl.ds(k0, tk), :]
        s = _dot_nt(q, kblk) * scale
        kpos = k0 + lax.broadcasted_iota(I32, (1, tk), 1)
        s = jnp.where(kpos <= qpos, s, -jnp.inf)
        _softmax_step(s, kblk[:, :kv_lora], m_sc, l_sc, acc_sc)
        return carry

    lax.fori_loop(0, nkb, body, 0)
    o = acc_sc[...] / l_sc[...]
    for h in range(MLA_HEADS):
        oh = o[h * tq:(h + 1) * tq].astype(BF16)
        o_ref[:, h * MLA_V_DIM:(h + 1) * MLA_V_DIM] = _dot(oh, wuv_ref[h]).astype(o_ref.dtype)


def _attn_prompt(q4, kv, wuv, batch, seq, tq, scale):
    kv_lora = wuv.shape[1]
    nq = seq // tq
    tk = _tile(seq, 256, BF16_SUBLANE)
    rows = MLA_HEADS * tq
    kern = functools.partial(_attn_prompt_kernel, tq=tq, tk=tk, scale=scale, kv_lora=kv_lora)
    return pl.pallas_call(
        kern,
        grid=(batch, nq),
        in_specs=[
            pl.BlockSpec((1, MLA_HEADS, tq, KV_W), lambda b, i: (b * nq + i, 0, 0, 0)),
            pl.BlockSpec((seq, KV_W), lambda b, i: (b, 0)),
            pl.BlockSpec(wuv.shape, lambda b, i: (0, 0, 0)),
        ],
        out_specs=pl.BlockSpec((tq, MLA_HEADS * MLA_V_DIM), lambda b, i: (b * nq + i, 0)),
        out_shape=jax.ShapeDtypeStruct((batch * seq, MLA_HEADS * MLA_V_DIM), BF16),
        scratch_shapes=[pltpu.VMEM((rows, 1), F32), pltpu.VMEM((rows, 1), F32), pltpu.VMEM((rows, kv_lora), F32)],
        compiler_params=_cparams(("parallel", "arbitrary")),
        name="mla_prompt_attention",
    )(q4, kv, wuv)


def _attn_sample_kernel(pt_ref, q_ref, kvn_ref, wuv_ref, *rest, pp, scale, t_valid, t_pad, kv_lora):
    ckv_refs, ckr_refs = rest[:pp], rest[pp:2 * pp]
    o_ref, kbuf, m_sc, l_sc, acc_sc = rest[2 * pp:]
    c = pl.program_id(1)
    page = ckv_refs[0].shape[2]

    @pl.when(c == 0)
    def _():
        m_sc[...] = jnp.full(m_sc.shape, -jnp.inf, F32)
        l_sc[...] = jnp.zeros(l_sc.shape, F32)
        acc_sc[...] = jnp.zeros(acc_sc.shape, F32)
        kbuf[:, kv_lora + ROPE_DIM:KV_W] = jnp.zeros((kbuf.shape[0], KV_W - kv_lora - ROPE_DIM), BF16)

    for i in range(pp):
        kbuf[i * page:(i + 1) * page, 0:kv_lora] = ckv_refs[i][0, 0].astype(BF16)
        kbuf[i * page:(i + 1) * page, kv_lora:kv_lora + ROPE_DIM] = ckr_refs[i][0, 0].astype(BF16)
    q = q_ref[0]
    k = kbuf[...]
    _softmax_step(_dot_nt(q, k) * scale, k[:, :kv_lora], m_sc, l_sc, acc_sc)

    @pl.when(c == pl.num_programs(1) - 1)
    def _():
        kn = kvn_ref[0]
        s = _dot_nt(q, kn) * scale
        qt = lax.rem(lax.broadcasted_iota(I32, (q.shape[0], 1), 0), t_pad)
        kt = lax.broadcasted_iota(I32, (1, t_pad), 1)
        s = jnp.where((kt <= qt) & (kt < t_valid), s, -jnp.inf)
        _softmax_step(s, kn[:, :kv_lora], m_sc, l_sc, acc_sc)
        o = acc_sc[...] / l_sc[...]
        for h in range(MLA_HEADS):
            oh = o[h * t_pad:(h + 1) * t_pad].astype(BF16)
            o_ref[0, :, h * MLA_V_DIM:(h + 1) * MLA_V_DIM] = _dot(oh, wuv_ref[h])


def _attn_sample(page_table, qs, kvn, wuv, cache_kv, cache_kr, layer, scale, t_valid):
    nseq, rows, _ = qs.shape
    t_pad = kvn.shape[1]
    npages = page_table.shape[1]
    page, kv_lora = cache_kv.shape[2], cache_kv.shape[3]
    pp = _tile(npages, 16, 1)
    kern = functools.partial(_attn_sample_kernel, pp=pp, scale=scale, t_valid=t_valid, t_pad=t_pad, kv_lora=kv_lora)

    def page_spec(width, i):
        return pl.BlockSpec((1, 1, page, width), lambda s, c, pt: (layer, pt[s, c * pp + i], 0, 0))

    grid_spec = pltpu.PrefetchScalarGridSpec(
        num_scalar_prefetch=1,
        grid=(nseq, npages // pp),
        in_specs=[
            pl.BlockSpec((1, rows, KV_W), lambda s, c, pt: (s, 0, 0)),
            pl.BlockSpec((1, t_pad, KV_W), lambda s, c, pt: (s, 0, 0)),
            pl.BlockSpec(wuv.shape, lambda s, c, pt: (0, 0, 0)),
        ] + [page_spec(kv_lora, i) for i in range(pp)] + [page_spec(ROPE_DIM, i) for i in range(pp)],
        out_specs=pl.BlockSpec((1, t_pad, MLA_HEADS * MLA_V_DIM), lambda s, c, pt: (s, 0, 0)),
        scratch_shapes=[
            pltpu.VMEM((pp * page, KV_W), BF16),
            pltpu.VMEM((rows, 1), F32),
            pltpu.VMEM((rows, 1), F32),
            pltpu.VMEM((rows, kv_lora), F32),
        ],
    )
    return pl.pallas_call(
        kern,
        grid_spec=grid_spec,
        out_shape=jax.ShapeDtypeStruct((nseq, t_pad, MLA_HEADS * MLA_V_DIM), F32),
        compiler_params=_cparams(("parallel", "arbitrary")),
        name="mla_sample_attention",
    )(page_table, qs, kvn, wuv, *([cache_kv] * pp), *([cache_kr] * pp))


def _mlstm_kernel(bias_ref, q_ref, k_ref, v_ref, mo_ref, gcol_ref, grow_ref, gmh_ref, c0_ref, n0_ref, m0_ref,
                  h_ref, c_ref, n_ref, m_ref, c_sc, n_sc, m_sc, *, chunk, nchunks, valid):
    hd = pl.program_id(1)
    b_i, b_f = bias_ref[0, hd], bias_ref[1, hd]
    c_sc[...] = c0_ref[0, 0]
    n_sc[...] = n0_ref[0]
    m_sc[...] = m0_ref[0]
    L = chunk
    li = lax.broadcasted_iota(I32, (L, L), 0)
    si = lax.broadcasted_iota(I32, (L, L), 1)
    causal = si <= li
    col_ok = lax.broadcasted_iota(I32, (L, 1), 0) < valid
    row_ok = lax.broadcasted_iota(I32, (1, L), 1) < valid
    kscale = MLSTM_QK_DIM ** -0.5

    def step(c, carry):
        r0 = pl.multiple_of(c * L, L)
        q = q_ref[pl.ds(r0, L), :]
        ks = k_ref[pl.ds(r0, L), :] * kscale
        v = v_ref[pl.ds(r0, L), :].astype(BF16)
        gc = gcol_ref[pl.ds(r0, L), :]
        gr = grow_ref[0, pl.ds(c, 1), :]
        ig_col = jnp.where(col_ok, gc[:, 0:1] + b_i, -jnp.inf)
        ig_row = jnp.where(row_ok, gr[:, 0:L] + b_i, -jnp.inf)
        fl_col = jnp.where(col_ok, jax.nn.log_sigmoid(gc[:, 1:2] + b_f), 0.0)
        fl_row = jnp.where(row_ok, jax.nn.log_sigmoid(gr[:, L:2 * L] + b_f), 0.0)
        b_col = jnp.sum(jnp.where(causal, fl_row, 0.0), axis=1, keepdims=True)
        b_row = jnp.sum(jnp.where(li <= si, fl_col, 0.0), axis=0, keepdims=True)
        m_prev = m_sc[:, 0:1]
        dmat = jnp.where(causal, b_col - b_row + ig_row, -jnp.inf)
        m_inter = b_col + m_prev
        m_t = jnp.maximum(m_inter, jnp.max(dmat, axis=1, keepdims=True))
        w_inter = jnp.exp(m_inter - m_t)
        qb = q.astype(BF16)
        s = _dot_nt(qb, ks.astype(BF16)) * jnp.exp(dmat - m_t)
        num = w_inter * _dot(qb, c_sc[...].astype(BF16)) + _dot(s.astype(BF16), v)
        den = w_inter * jnp.sum(q * n_sc[...], axis=1, keepdims=True) + jnp.sum(s, axis=1, keepdims=True)
        hh = num / jnp.maximum(jnp.abs(den), jnp.exp(-m_t))
        y = _rms(hh, gmh_ref[...])
        h_ref[pl.ds(r0, L), :] = (y * jax.nn.sigmoid(mo_ref[pl.ds(r0, L), :])).astype(h_ref.dtype)
        b_last = jnp.sum(fl_row, axis=1, keepdims=True)
        w_end_col = b_last - b_col + ig_col
        w_end_row = b_last - b_row + ig_row
        m_new = jnp.maximum(b_last + m_prev, jnp.max(w_end_row, axis=1, keepdims=True))
        a_state = jnp.exp(b_last + m_prev - m_new)
        ka = ks * jnp.exp(w_end_col - m_new)
        c_sc[...] = a_state * c_sc[...] + _dot_tn(ka.astype(BF16), v)
        n_sc[...] = a_state * n_sc[...] + jnp.sum(ka, axis=0, keepdims=True)
        m_sc[...] = jnp.broadcast_to(m_new, m_sc.shape)
        return carry

    lax.fori_loop(0, nchunks, step, 0)
    c_ref[0, 0] = c_sc[...]
    n_ref[0] = n_sc[...]
    m_ref[0] = m_sc[...]


def _mlstm(z2, gates, bias, g_mh, c0, n0, m0, batch, t_rows, chunk, valid, zo):
    H, DK, DV = MLSTM_HEADS, MLSTM_QK_DIM, MLSTM_V_DIM
    nchunks = t_rows // chunk
    gcol = gates.transpose(0, 3, 1, 2).reshape(batch * H * t_rows, 2)
    grow = gates.reshape(batch, nchunks, chunk, 2, H).transpose(0, 4, 1, 3, 2).reshape(batch * H, nchunks, 2 * chunk)
    n0r = n0.reshape(batch * H, 1, DK)
    m0r = jnp.broadcast_to(m0.reshape(batch * H, 1, 1).astype(F32), (batch * H, 1, LANE))
    kern = functools.partial(_mlstm_kernel, chunk=chunk, nchunks=nchunks, valid=valid)
    grid_spec = pltpu.PrefetchScalarGridSpec(
        num_scalar_prefetch=1,
        grid=(batch, H),
        in_specs=[
            pl.BlockSpec((t_rows, DK), lambda b, h, bias: (b, zo["mq"] // DK + h)),
            pl.BlockSpec((t_rows, DK), lambda b, h, bias: (b, zo["mk"] // DK + h)),
            pl.BlockSpec((t_rows, DV), lambda b, h, bias: (b, zo["mv"] // DV + h)),
            pl.BlockSpec((t_rows, DV), lambda b, h, bias: (b, zo["mo"] // DV + h)),
            pl.BlockSpec((t_rows, 2), lambda b, h, bias: (b * H + h, 0)),
            pl.BlockSpec((1, nchunks, 2 * chunk), lambda b, h, bias: (b * H + h, 0, 0)),
            pl.BlockSpec((1, DV), lambda b, h, bias: (0, h)),
            pl.BlockSpec((1, 1, DK, DV), lambda b, h, bias: (b, h, 0, 0)),
            pl.BlockSpec((1, 1, DK), lambda b, h, bias: (b * H + h, 0, 0)),
            pl.BlockSpec((1, 1, LANE), lambda b, h, bias: (b * H + h, 0, 0)),
        ],
        out_specs=[
            pl.BlockSpec((t_rows, DV), lambda b, h, bias: (b, h)),
            pl.BlockSpec((1, 1, DK, DV), lambda b, h, bias: (b, h, 0, 0)),
            pl.BlockSpec((1, 1, DK), lambda b, h, bias: (b * H + h, 0, 0)),
            pl.BlockSpec((1, 1, LANE), lambda b, h, bias: (b * H + h, 0, 0)),
        ],
        scratch_shapes=[pltpu.VMEM((DK, DV), F32), pltpu.VMEM((1, DK), F32), pltpu.VMEM((1, LANE), F32)],
    )
    hg, c, n, m = pl.pallas_call(
        kern,
        grid_spec=grid_spec,
        out_shape=[
            jax.ShapeDtypeStruct((batch * t_rows, H * DV), BF16),
            jax.ShapeDtypeStruct((batch, H, DK, DV), F32),
            jax.ShapeDtypeStruct((batch * H, 1, DK), F32),
            jax.ShapeDtypeStruct((batch * H, 1, LANE), F32),
        ],
        compiler_params=_cparams(("parallel", "parallel")),
        name="mlstm",
    )(bias, z2, z2, z2, z2, gcol, grow, g_mh.reshape(1, H * DV), c0, n0r, m0r)
    return hg, c, n.reshape(batch, H, DK), m[:, 0, 0].reshape(batch, H)


def _take_top(work, src, nsel):
    nrow = work.shape[0]
    row = lax.broadcasted_iota(I32, work.shape, 0)
    kept = jnp.full(work.shape, -jnp.inf, F32)
    rank = jnp.full(work.shape, NOT_A_RANK, F32)
    vals, idxs = [], []
    for a in range(nsel):
        m = jnp.max(work, axis=0, keepdims=True)
        am = jnp.min(jnp.where(work == m, row, nrow), axis=0, keepdims=True)
        hit = row == am
        vals.append(m)
        idxs.append(am)
        kept = jnp.where(hit, src, kept)
        rank = jnp.where(hit, float(a), rank)
        work = jnp.where(hit, -jnp.inf, work)
    return vals, idxs, kept, rank


def _peer_select_kernel(pq_ref, keys_ref, s1m_ref, e1_ref, pt_ref, s2m_ref, e2_ref, rb_ref, tau_ref):
    K = PEER_TOPK
    for h in range(PEER_HEADS):
        sides = []
        for p in range(2):
            qs = pq_ref[:, (2 * h + p) * LANE:(2 * h + p + 1) * LANE].astype(BF16)
            st = _dot_nt(keys_ref[2 * h + p], qs)
            sides.append(_take_top(st, st, K))
        (v1, _, s1m, ra), (v2, _, s2m, rb) = sides
        v2all = jnp.concatenate(v2, axis=0)
        cand = jnp.concatenate([v1[a] + v2all for a in range(K)], axis=0)
        top, pos, _, _ = _take_top(cand, cand, K)
        zsum = jnp.ones_like(top[0])
        for k in range(1, K):
            zsum = zsum + jnp.exp(top[k] - top[0])
        s1m_ref[h] = s1m
        e1_ref[h] = jnp.exp(s1m - v1[0]) / zsum
        pt_ref[h] = pos[K - 1].astype(F32) - float(K) * ra
        s2m_ref[h] = s2m
        e2_ref[h] = jnp.exp(s2m - v2[0])
        rb_ref[h] = rb
        tau_ref[h:h + 1, :] = top[K - 1]


def _peer_select(pq, keys, tm):
    n = pq.shape[0]
    big = jax.ShapeDtypeStruct((PEER_HEADS, PEER_N_KEYS, n), F32)
    bspec = pl.BlockSpec((PEER_HEADS, PEER_N_KEYS, tm), lambda i: (0, 0, i))
    return pl.pallas_call(
        _peer_select_kernel,
        grid=(n // tm,),
        in_specs=[pl.BlockSpec((tm, pq.shape[1]), lambda i: (i, 0)), pl.BlockSpec(keys.shape, lambda i: (0, 0, 0))],
        out_specs=[bspec] * 6 + [pl.BlockSpec((PEER_HEADS, tm), lambda i: (0, i))],
        out_shape=[big] * 6 + [jax.ShapeDtypeStruct((PEER_HEADS, n), F32)],
        compiler_params=_cparams(("parallel",)),
        name="peer_select",
    )(pq, keys)


def _peer_expert_kernel(xn_ref, wd_ref, wu_ref, s1m_ref, e1_ref, pt_ref, s2m_ref, e2_ref, rb_ref, tau_ref, o_ref, *, krows):
    @pl.when(pl.program_id(1) == 0)
    def _():
        o_ref[...] = jnp.zeros(o_ref.shape, F32)

    a_t = _dot_nt(wd_ref[...], xn_ref[...])
    gel = 0.5 * a_t * (1.0 + lax.erf(a_t * INV_SQRT2))
    gates = []
    for r in range(krows):
        acc = None
        for h in range(PEER_HEADS):
            c = s1m_ref[h, 0, r:r + 1, :] + s2m_ref[h]
            tau = tau_ref[h:h + 1, :]
            sel = (c > tau) | ((c == tau) & (rb_ref[h] <= pt_ref[h, 0, r:r + 1, :]))
            term = jnp.where(sel, e2_ref[h], 0.0) * e1_ref[h, 0, r:r + 1, :]
            acc = term if acc is None else acc + term
        gates.append(acc)
    coef = (jnp.concatenate(gates, axis=0) * gel).astype(BF16)
    o_ref[...] += _dot_tn(coef, wu_ref[...])


def _peer_experts(xn, wd, wu, sel, tm, te):
    n, d = xn.shape
    ne = wd.shape[0]
    krows = te // PEER_N_KEYS
    s1m, e1, pt, s2m, e2, rb, tau = sel
    nk = PEER_N_KEYS // krows
    side1 = [a.reshape(PEER_HEADS, nk, krows, n) for a in (s1m, e1, pt)]
    spec1 = pl.BlockSpec((PEER_HEADS, 1, krows, tm), lambda j, i: (0, i, 0, j))
    spec2 = pl.BlockSpec((PEER_HEADS, PEER_N_KEYS, tm), lambda j, i: (0, 0, j))
    return pl.pallas_call(
        functools.partial(_peer_expert_kernel, krows=krows),
        grid=(n // tm, ne // te),
        in_specs=[
            pl.BlockSpec((tm, d), lambda j, i: (j, 0)),
            pl.BlockSpec((te, d), lambda j, i: (i, 0)),
            pl.BlockSpec((te, d), lambda j, i: (i, 0)),
            spec1, spec1, spec1, spec2, spec2, spec2,
            pl.BlockSpec((PEER_HEADS, tm), lambda j, i: (0, j)),
        ],
        out_specs=pl.BlockSpec((tm, d), lambda j, i: (j, 0)),
        out_shape=jax.ShapeDtypeStruct((n, d), F32),
        compiler_params=_cparams(("parallel", "arbitrary")),
        name="peer_experts",
    )(xn, wd, wu, *side1, s2m, e2, rb, tau)


def _rope_table(pos):
    inv = ROPE_THETA ** (-jnp.arange(0, ROPE_DIM, 2, dtype=F32) / ROPE_DIM)
    ang = pos.astype(F32)[:, None] * inv[None, :]
    reps = LANE // (ROPE_DIM // 2)
    return jnp.concatenate([jnp.tile(jnp.cos(ang), (1, reps)), jnp.tile(jnp.sin(ang), (1, reps))], axis=1)


def _rot_cols(w):
    half = ROPE_DIM // 2
    return jnp.concatenate([-w[..., half:], w[..., :half]], axis=-1)


def _pad_cols(w, width):
    return jnp.pad(w, [(0, 0)] * (w.ndim - 1) + [(0, width - w.shape[-1])])


def kernel(x_prompt, x_sample, p_prompt, p_sample, cache_kv_latent, cache_k_rope, state_mlstm_C, state_mlstm_n, state_mlstm_m, page_table, g_mix_norm, w_in, g_q_latent, w_uq, g_kv_latent, w_uk, w_uv, b_igate, b_fgate, g_mlstm_head, w_out, g_ffn_norm, peer_w_query, peer_sub_keys, peer_w_down, peer_w_up, g_ple_norm, w_ple, w_ple_gate, g_final_norm):
    B, S, D = x_prompt.shape
    DB, T, _ = x_sample.shape
    depth = w_in.shape[0]
    q_lora, kv_lora = g_q_latent.shape[1], g_kv_latent.shape[1]
    H, DK, DV = MLSTM_HEADS, MLSTM_QK_DIM, MLSTM_V_DIM
    n_p, n_s = B * S, DB * T
    n = n_p + n_s
    n_past = page_table.shape[1] * PAGE_SIZE
    scale = (NOPE_DIM + ROPE_DIM) ** -0.5
    t_pad = SUBLANE * ((T + SUBLANE - 1) // SUBLANE)

    zo, off = {}, 0
    for name, width in (("cq", q_lora), ("ckv", kv_lora), ("mq", H * DK), ("mk", H * DK), ("mv", H * DV),
                        ("mo", H * DV), ("kr", 2 * LANE), ("gate", LANE)):
        zo[name] = off
        off += width
    zw = 1024 * ((off + 1023) // 1024)

    cs = jnp.concatenate([jnp.tile(_rope_table(jnp.arange(S, dtype=I32)), (B, 1)),
                          jnp.tile(_rope_table(n_past + jnp.arange(T, dtype=I32)), (DB, 1))], axis=0)
    tmq = _tile(math.gcd(S, n_s), 128, BF16_SUBLANE)
    chunk_p = math.gcd(S, MLSTM_CHUNK)

    x = jnp.concatenate([x_prompt.reshape(n_p, D), x_sample.reshape(n_s, D)], axis=0)
    outs = {k: [] for k in ("ckv_p", "kr_p", "ckv_s", "kr_s", "C_p", "n_p", "m_p", "C_s", "n_s", "m_s")}
    for l in range(depth):
        sizes = (q_lora, kv_lora, ROPE_DIM, H * DK, H * DK, H * DV, H * DV, H, H)
        parts, st = [], 0
        for sz in sizes:
            parts.append(w_in[l][:, st:st + sz])
            st += sz
        wcq, wckv, wkr, wmq, wmk, wmv, wmo, wmi, wmf = parts
        w_z = jnp.concatenate([wcq, wckv, wmq, wmk, wmv, wmo, _pad_cols(wkr, LANE), _pad_cols(_rot_cols(wkr), LANE),
                               wmi, wmf], axis=1)
        w_z = _pad_cols(w_z, zw).astype(BF16)
        wq3 = w_uq[l].reshape(q_lora, MLA_HEADS, NOPE_DIM + ROPE_DIM)
        wn = wq3[:, :, :NOPE_DIM].reshape(q_lora, MLA_HEADS * NOPE_DIM).astype(BF16)
        wr = _pad_cols(wq3[:, :, NOPE_DIM:], LANE).reshape(q_lora, MLA_HEADS * LANE).astype(BF16)
        wrr = _pad_cols(_rot_cols(wq3[:, :, NOPE_DIM:]), LANE).reshape(q_lora, MLA_HEADS * LANE).astype(BF16)
        wuk = w_uk[l].reshape(kv_lora, MLA_HEADS, NOPE_DIM).transpose(1, 2, 0).astype(BF16)
        wuv = w_uv[l].reshape(kv_lora, MLA_HEADS, MLA_V_DIM).transpose(1, 0, 2).astype(BF16)
        bias = jnp.stack([b_igate[l], b_fgate[l]]).astype(F32)

        hn = _rmsnorm(x, g_mix_norm[l], BF16)
        z = _matmul(hn, w_z)

        q4, ckv, kr, kvb = _qprep(z, cs, g_q_latent[l], g_kv_latent[l], wn, wr, wrr, wuk, tmq, zo)
        o_mla_p = _attn_prompt(q4, kvb, wuv, B, S, tmq, scale)
        nblk_s = n_s // tmq
        qs = q4[n_p // tmq:].transpose(1, 0, 2, 3).reshape(MLA_HEADS, DB, T, KV_W).transpose(1, 0, 2, 3)
        qs = jnp.pad(qs, ((0, 0), (0, 0), (0, t_pad - T), (0, 0))).reshape(DB, MLA_HEADS * t_pad, KV_W)
        kvn = jnp.pad(kvb[n_p:].reshape(DB, T, KV_W), ((0, 0), (0, t_pad - T), (0, 0)))
        o_mla_s = _attn_sample(page_table, qs, kvn, wuv, cache_kv_latent, cache_k_rope, l, scale, T)
        o_mla = jnp.concatenate([o_mla_p, o_mla_s[:, :T].reshape(n_s, -1).astype(BF16)], axis=0)

        gz = z[:, zo["gate"]:zo["gate"] + 2 * H]
        c0p = jnp.zeros((B, H, DK, DV), F32)
        n0p = jnp.zeros((B, H, DK), F32)
        m0p = jnp.full((B, H), M_EMPTY, F32)
        hg_p, C_p, nn_p, mm_p = _mlstm(z, gz[:n_p].reshape(B, S, 2, H), bias, g_mlstm_head[l], c0p, n0p, m0p,
                                       B, S, chunk_p, chunk_p, zo)
        z_s = jnp.pad(z[n_p:].reshape(DB, T, zw), ((0, 0), (0, t_pad - T), (0, 0))).reshape(DB * t_pad, zw)
        g_s = jnp.pad(gz[n_p:].reshape(DB, T, 2, H), ((0, 0), (0, t_pad - T), (0, 0), (0, 0)))
        hg_s, C_s, nn_s, mm_s = _mlstm(z_s, g_s, bias, g_mlstm_head[l], state_mlstm_C[l].astype(F32),
                                       state_mlstm_n[l].astype(F32), state_mlstm_m[l], DB, t_pad, t_pad, T, zo)
        o_ml = jnp.concatenate([hg_p, hg_s.reshape(DB, t_pad, H * DV)[:, :T].reshape(n_s, H * DV)], axis=0)

        x1 = _outproj(o_mla, o_ml, w_out[l].astype(BF16), x)

        xn = _rmsnorm(x1, g_ffn_norm[l], BF16)
        pq = _matmul(xn, peer_w_query[l].astype(BF16))
        keys = peer_sub_keys[l].reshape(2 * PEER_HEADS, PEER_N_KEYS, -1).astype(BF16)
        tms = _tile(n, 256, LANE)
        sel = _peer_select(pq, keys, tms)
        tmx = _tile(n, 512, LANE)
        peer_out = _peer_experts(xn, peer_w_down[l].astype(BF16), peer_w_up[l].astype(BF16), sel, tmx, 2 * PEER_N_KEYS)

        x2, xg = _add_rmsnorm(x1, peer_out, g_ple_norm[l])
        pcat = jnp.concatenate([p_prompt[l].reshape(n_p, -1), p_sample[l].reshape(n_s, -1)], axis=0).astype(BF16)
        x = _ple(xg, w_ple_gate[l].astype(BF16), pcat, w_ple[l].astype(BF16), x2)

        outs["ckv_p"].append(ckv[:n_p].reshape(B, S, kv_lora))
        outs["kr_p"].append(kr[:n_p].reshape(B, S, ROPE_DIM))
        outs["ckv_s"].append(ckv[n_p:].reshape(DB, T, kv_lora))
        outs["kr_s"].append(kr[n_p:].reshape(DB, T, ROPE_DIM))
        outs["C_p"].append(C_p)
        outs["n_p"].append(nn_p)
        outs["m_p"].append(mm_p)
        outs["C_s"].append(C_s.astype(state_mlstm_C.dtype))
        outs["n_s"].append(nn_s.astype(state_mlstm_n.dtype))
        outs["m_s"].append(mm_s.astype(state_mlstm_m.dtype))

    y = _rmsnorm(x, g_final_norm, F32)
    st = {k: jnp.stack(v) for k, v in outs.items()}
    return (y[:n_p].reshape(B, S, D), y[n_p:].reshape(DB, T, D), st["ckv_p"], st["kr_p"], st["ckv_s"], st["kr_s"],
            st["C_p"], st["n_p"], st["m_p"], st["C_s"], st["n_s"], st["m_s"])
```

```python
import functools
import math

import jax
import jax.numpy as jnp
from jax import lax
from jax.experimental import pallas as pl
from jax.experimental.pallas import tpu as pltpu

F32, BF16, I32 = jnp.float32, jnp.bfloat16, jnp.int32

MLA_HEADS = 16
NOPE_DIM = 128
ROPE_DIM = 64
MLA_V_DIM = 128
ROPE_THETA = 10000.0
MLSTM_HEADS = 8
MLSTM_QK_DIM = 128
MLSTM_V_DIM = 256
MLSTM_CHUNK = 64
M_EMPTY = -1e30
PEER_HEADS = 8
PEER_N_KEYS = 128
PEER_TOPK = 16
PAGE_SIZE = 128
EPS = 1e-6
INV_SQRT2 = 0.7071067811865476

LANE = 128
SUBLANE = 8
BF16_SUBLANE = 16
VMEM_LIMIT_BYTES = 60 * 1024 * 1024

KV_W = 640
NOT_A_RANK = 1.0e9
ORDER_MAX = 2 ** 30
CAND_ROW_BLOCKS = 4
MLSTM_HEADS_PER_STEP_PROMPT = 2
MLSTM_HEADS_PER_STEP_SAMPLE = 8


def _tile(n, cap, mult):
    best = None
    d = mult
    while d <= min(n, cap):
        if n % d == 0:
            best = d
        d += mult
    return n if best is None else best


def _cparams(sem):
    return pltpu.CompilerParams(dimension_semantics=sem, vmem_limit_bytes=VMEM_LIMIT_BYTES)


def _dot(a, b):
    return jnp.dot(a, b, preferred_element_type=F32)


def _dot_nt(a, b):
    return lax.dot_general(a, b, (((1,), (1,)), ((), ())), preferred_element_type=F32)


def _dot_tn(a, b):
    return lax.dot_general(a, b, (((0,), (0,)), ((), ())), preferred_element_type=F32)


def _rms(x, g):
    return x * lax.rsqrt(jnp.mean(x * x, axis=-1, keepdims=True) + EPS) * g


def _norm_kernel(x_ref, g_ref, o_ref):
    o_ref[...] = _rms(x_ref[...], g_ref[...]).astype(o_ref.dtype)


def _rmsnorm(x, g, out_dtype):
    n, d = x.shape
    tm = _tile(n, 256, BF16_SUBLANE)
    return pl.pallas_call(
        _norm_kernel,
        grid=(n // tm,),
        in_specs=[pl.BlockSpec((tm, d), lambda i: (i, 0)), pl.BlockSpec((1, d), lambda i: (0, 0))],
        out_specs=pl.BlockSpec((tm, d), lambda i: (i, 0)),
        out_shape=jax.ShapeDtypeStruct((n, d), out_dtype),
        compiler_params=_cparams(("parallel",)),
        name="rmsnorm",
    )(x, g.reshape(1, d))


def _add_norm_kernel(a_ref, b_ref, g_ref, s_ref, o_ref):
    s = a_ref[...] + b_ref[...]
    s_ref[...] = s
    o_ref[...] = _rms(s, g_ref[...]).astype(o_ref.dtype)


def _add_rmsnorm(a, b, g):
    n, d = a.shape
    tm = _tile(n, 128, BF16_SUBLANE)
    row = pl.BlockSpec((tm, d), lambda i: (i, 0))
    return pl.pallas_call(
        _add_norm_kernel,
        grid=(n // tm,),
        in_specs=[row, row, pl.BlockSpec((1, d), lambda i: (0, 0))],
        out_specs=[row, row],
        out_shape=[jax.ShapeDtypeStruct((n, d), F32), jax.ShapeDtypeStruct((n, d), BF16)],
        compiler_params=_cparams(("parallel",)),
        name="add_rmsnorm",
    )(a, b, g.reshape(1, d))


def _mm_kernel(a_ref, b_ref, o_ref):
    o_ref[...] = _dot(a_ref[...], b_ref[...]).astype(o_ref.dtype)


def _matmul(a, b, out_dtype=F32):
    m, k = a.shape
    n = b.shape[1]
    tm = _tile(m, 1088, BF16_SUBLANE)
    tn = _tile(n, 1024, LANE)
    return pl.pallas_call(
        _mm_kernel,
        grid=(m // tm, n // tn),
        in_specs=[pl.BlockSpec((tm, k), lambda i, j: (i, 0)), pl.BlockSpec((k, tn), lambda i, j: (0, j))],
        out_specs=pl.BlockSpec((tm, tn), lambda i, j: (i, j)),
        out_shape=jax.ShapeDtypeStruct((m, n), out_dtype),
        compiler_params=_cparams(("parallel", "parallel")),
        name="matmul",
    )(a, b)


def _outproj_kernel(a1_ref, a2_ref, b1_ref, b2_ref, r_ref, o_ref):
    o_ref[...] = r_ref[...] + (_dot(a1_ref[...], b1_ref[...]) + _dot(a2_ref[...], b2_ref[...]))


def _outproj(a1, a2, w, r):
    m, k1 = a1.shape
    k2 = a2.shape[1]
    n = w.shape[1]
    tm = _tile(m, 1088, BF16_SUBLANE)
    tn = _tile(n, 512, LANE)
    nb1 = k1 // k2
    return pl.pallas_call(
        _outproj_kernel,
        grid=(m // tm, n // tn),
        in_specs=[
            pl.BlockSpec((tm, k1), lambda i, j: (i, 0)),
            pl.BlockSpec((tm, k2), lambda i, j: (i, 0)),
            pl.BlockSpec((k1, tn), lambda i, j: (0, j)),
            pl.BlockSpec((k2, tn), lambda i, j: (nb1, j)),
            pl.BlockSpec((tm, tn), lambda i, j: (i, j)),
        ],
        out_specs=pl.BlockSpec((tm, tn), lambda i, j: (i, j)),
        out_shape=jax.ShapeDtypeStruct((m, n), F32),
        compiler_params=_cparams(("parallel", "parallel")),
        name="outproj",
    )(a1, a2, w, w, r)


def _ple_kernel(xg_ref, wg_ref, p_ref, wp_ref, x_ref, o_ref):
    gate = _dot(xg_ref[...], wg_ref[...])
    o_ref[...] = x_ref[...] + _dot(p_ref[...], wp_ref[...]) * jax.nn.sigmoid(gate)


def _ple(xg, w_gate, p, w_ple, x):
    m, k = xg.shape
    n = w_gate.shape[1]
    kp = p.shape[1]
    tm = _tile(m, 1088, BF16_SUBLANE)
    tn = _tile(n, 512, LANE)
    return pl.pallas_call(
        _ple_kernel,
        grid=(m // tm, n // tn),
        in_specs=[
            pl.BlockSpec((tm, k), lambda i, j: (i, 0)),
            pl.BlockSpec((k, tn), lambda i, j: (0, j)),
            pl.BlockSpec((tm, kp), lambda i, j: (i, 0)),
            pl.BlockSpec((kp, tn), lambda i, j: (0, j)),
            pl.BlockSpec((tm, tn), lambda i, j: (i, j)),
        ],
        out_specs=pl.BlockSpec((tm, tn), lambda i, j: (i, j)),
        out_shape=jax.ShapeDtypeStruct((m, n), F32),
        compiler_params=_cparams(("parallel", "parallel")),
        name="ple",
    )(xg, w_gate, p, w_ple, x)


def _qprep_kernel(cq_ref, ckv_ref, krk_ref, cs_ref, gq_ref, gkv_ref, wn_ref, wr_ref, wrr_ref, wuk_ref,
                  q_ref, ckv_out_ref, kr_out_ref, kv_ref):
    cqn = _rms(cq_ref[...], gq_ref[...]).astype(BF16)
    cs = cs_ref[...]
    cos1, sin1 = cs[:, :LANE], cs[:, LANE:]
    cos_t = jnp.concatenate([cos1] * MLA_HEADS, axis=1)
    sin_t = jnp.concatenate([sin1] * MLA_HEADS, axis=1)
    qn = _dot(cqn, wn_ref[...])
    qr = _dot(cqn, wr_ref[...]) * cos_t + _dot(cqn, wrr_ref[...]) * sin_t
    for h in range(MLA_HEADS):
        sl = slice(h * LANE, (h + 1) * LANE)
        qa = _dot(qn[:, sl].astype(BF16), wuk_ref[h])
        q_ref[0, h, :, 0:512] = qa.astype(BF16)
        q_ref[0, h, :, 512:KV_W] = qr[:, sl].astype(BF16)
    c = _rms(ckv_ref[...], gkv_ref[...])
    ckv_out_ref[...] = c
    krk = krk_ref[...]
    kr = krk[:, :LANE] * cos1 + krk[:, LANE:] * sin1
    kr_out_ref[...] = kr[:, :ROPE_DIM]
    kv_ref[:, 0:512] = c.astype(BF16)
    kv_ref[:, 512:KV_W] = kr.astype(BF16)


def _qprep(z, cs, g_q, g_kv, wn, wr, wrr, wuk, tm, zo):
    n = z.shape[0]
    q_lora = wn.shape[0]
    kv_lora = wuk.shape[2]
    const2 = lambda i: (0, 0)
    return pl.pallas_call(
        _qprep_kernel,
        grid=(n // tm,),
        in_specs=[
            pl.BlockSpec((tm, q_lora), lambda i: (i, zo["cq"] // q_lora)),
            pl.BlockSpec((tm, kv_lora), lambda i: (i, zo["ckv"] // kv_lora)),
            pl.BlockSpec((tm, 2 * LANE), lambda i: (i, zo["kr"] // (2 * LANE))),
            pl.BlockSpec((tm, 2 * LANE), lambda i: (i, 0)),
            pl.BlockSpec((1, q_lora), const2),
            pl.BlockSpec((1, kv_lora), const2),
            pl.BlockSpec(wn.shape, const2),
            pl.BlockSpec(wr.shape, const2),
            pl.BlockSpec(wrr.shape, const2),
            pl.BlockSpec(wuk.shape, lambda i: (0, 0, 0)),
        ],
        out_specs=[
            pl.BlockSpec((1, MLA_HEADS, tm, KV_W), lambda i: (i, 0, 0, 0)),
            pl.BlockSpec((tm, kv_lora), lambda i: (i, 0)),
            pl.BlockSpec((tm, ROPE_DIM), lambda i: (i, 0)),
            pl.BlockSpec((tm, KV_W), lambda i: (i, 0)),
        ],
        out_shape=[
            jax.ShapeDtypeStruct((n // tm, MLA_HEADS, tm, KV_W), BF16),
            jax.ShapeDtypeStruct((n, kv_lora), F32),
            jax.ShapeDtypeStruct((n, ROPE_DIM), F32),
            jax.ShapeDtypeStruct((n, KV_W), BF16),
        ],
        compiler_params=_cparams(("parallel",)),
        name="mla_qprep",
    )(z, z, z, cs, g_q.reshape(1, -1), g_kv.reshape(1, -1), wn, wr, wrr, wuk)


def _softmax_step(s, v, m_sc, l_sc, acc_sc):
    m_old = m_sc[...]
    m_new = jnp.maximum(m_old, jnp.max(s, axis=-1, keepdims=True))
    alpha = jnp.exp(m_old - m_new)
    p = jnp.exp(s - m_new)
    l_sc[...] = alpha * l_sc[...] + jnp.sum(p, axis=-1, keepdims=True)
    acc_sc[...] = alpha * acc_sc[...] + _dot(p.astype(BF16), v)
    m_sc[...] = m_new


def _attn_prompt_kernel(q_ref, kv_ref, wuv_ref, o_ref, m_sc, l_sc, acc_sc, *, tq, tk, scale, kv_lora):
    qi = pl.program_id(1)
    rows = MLA_HEADS * tq
    q = q_ref[0].reshape(rows, KV_W)
    m_sc[...] = jnp.full(m_sc.shape, -jnp.inf, F32)
    l_sc[...] = jnp.zeros(l_sc.shape, F32)
    acc_sc[...] = jnp.zeros(acc_sc.shape, F32)
    nfull = (qi * tq) // tk

    def block(j, masked):
        k0 = pl.multiple_of(j * tk, tk)
        kblk = kv_ref[pl.ds(k0, tk), :]
        s = _dot_nt(q, kblk) * scale
        if masked:
            qpos = qi * tq + lax.rem(lax.broadcasted_iota(I32, (rows, 1), 0), tq)
            kpos = k0 + lax.broadcasted_iota(I32, (1, tk), 1)
            s = jnp.where(kpos <= qpos, s, -jnp.inf)
        _softmax_step(s, kblk[:, :kv_lora], m_sc, l_sc, acc_sc)

    def body(j, carry):
        block(j, False)
        return carry

    lax.fori_loop(0, nfull, body, 0)
    block(nfull, True)
    o = acc_sc[...] / l_sc[...]
    for h in range(MLA_HEADS):
        oh = o[h * tq:(h + 1) * tq].astype(BF16)
        o_ref[:, h * MLA_V_DIM:(h + 1) * MLA_V_DIM] = _dot(oh, wuv_ref[h]).astype(o_ref.dtype)


def _attn_prompt(q4, kv, wuv, batch, seq, tq, scale):
    kv_lora = wuv.shape[1]
    nq = seq // tq
    tk = _tile(seq, 512, tq)
    rows = MLA_HEADS * tq
    kern = functools.partial(_attn_prompt_kernel, tq=tq, tk=tk, scale=scale, kv_lora=kv_lora)
    return pl.pallas_call(
        kern,
        grid=(batch, nq),
        in_specs=[
            pl.BlockSpec((1, MLA_HEADS, tq, KV_W), lambda b, i: (b * nq + i, 0, 0, 0)),
            pl.BlockSpec((seq, KV_W), lambda b, i: (b, 0)),
            pl.BlockSpec(wuv.shape, lambda b, i: (0, 0, 0)),
        ],
        out_specs=pl.BlockSpec((tq, MLA_HEADS * MLA_V_DIM), lambda b, i: (b * nq + i, 0)),
        out_shape=jax.ShapeDtypeStruct((batch * seq, MLA_HEADS * MLA_V_DIM), BF16),
        scratch_shapes=[pltpu.VMEM((rows, 1), F32), pltpu.VMEM((rows, 1), F32), pltpu.VMEM((rows, kv_lora), F32)],
        compiler_params=_cparams(("parallel", "arbitrary")),
        name="mla_prompt_attention",
    )(q4, kv, wuv)


def _attn_sample_kernel(pt_ref, q_ref, kvn_ref, wuv_ref, *rest, pp, scale, t_valid, t_pad, kv_lora):
    ckv_refs, ckr_refs = rest[:pp], rest[pp:2 * pp]
    o_ref, cbuf, rbuf, m_sc, l_sc, acc_sc = rest[2 * pp:]
    c = pl.program_id(1)
    page = ckv_refs[0].shape[2]

    @pl.when(c == 0)
    def _():
        m_sc[...] = jnp.full(m_sc.shape, -jnp.inf, F32)
        l_sc[...] = jnp.zeros(l_sc.shape, F32)
        acc_sc[...] = jnp.zeros(acc_sc.shape, F32)

    for i in range(pp):
        cbuf[i * page:(i + 1) * page, :] = ckv_refs[i][0, 0].astype(BF16)
        rbuf[:, i * page:(i + 1) * page] = ckr_refs[i][0, 0].astype(BF16)
    q = q_ref[0]
    kc = cbuf[...]
    s = _dot_nt(q[:, :kv_lora], kc) + _dot(q[:, kv_lora:kv_lora + ROPE_DIM], rbuf[...])
    _softmax_step(s * scale, kc, m_sc, l_sc, acc_sc)

    @pl.when(c == pl.num_programs(1) - 1)
    def _():
        kn = kvn_ref[0]
        s = _dot_nt(q, kn) * scale
        qt = lax.rem(lax.broadcasted_iota(I32, (q.shape[0], 1), 0), t_pad)
        kt = lax.broadcasted_iota(I32, (1, t_pad), 1)
        s = jnp.where((kt <= qt) & (kt < t_valid), s, -jnp.inf)
        _softmax_step(s, kn[:, :kv_lora], m_sc, l_sc, acc_sc)
        o = acc_sc[...] / l_sc[...]
        for h in range(MLA_HEADS):
            oh = o[h * t_pad:(h + 1) * t_pad].astype(BF16)
            o_ref[0, :, h * MLA_V_DIM:(h + 1) * MLA_V_DIM] = _dot(oh, wuv_ref[h])


def _attn_sample(page_table, qs, kvn, wuv, cache_kv, cache_kr_t, layer, scale, t_valid):
    nseq, rows, _ = qs.shape
    t_pad = kvn.shape[1]
    npages = page_table.shape[1]
    page, kv_lora = cache_kv.shape[2], cache_kv.shape[3]
    pp = _tile(npages, 32, 1)
    kern = functools.partial(_attn_sample_kernel, pp=pp, scale=scale, t_valid=t_valid, t_pad=t_pad, kv_lora=kv_lora)

    def page_spec(shape, i):
        return pl.BlockSpec((1, 1) + shape, lambda s, c, pt: (layer, pt[s, c * pp + i], 0, 0))

    grid_spec = pltpu.PrefetchScalarGridSpec(
        num_scalar_prefetch=1,
        grid=(nseq, npages // pp),
        in_specs=[
            pl.BlockSpec((1, rows, KV_W), lambda s, c, pt: (s, 0, 0)),
            pl.BlockSpec((1, t_pad, KV_W), lambda s, c, pt: (s, 0, 0)),
            pl.BlockSpec(wuv.shape, lambda s, c, pt: (0, 0, 0)),
        ] + [page_spec((page, kv_lora), i) for i in range(pp)] + [page_spec((ROPE_DIM, page), i) for i in range(pp)],
        out_specs=pl.BlockSpec((1, t_pad, MLA_HEADS * MLA_V_DIM), lambda s, c, pt: (s, 0, 0)),
        scratch_shapes=[
            pltpu.VMEM((pp * page, kv_lora), BF16),
            pltpu.VMEM((ROPE_DIM, pp * page), BF16),
            pltpu.VMEM((rows, 1), F32),
            pltpu.VMEM((rows, 1), F32),
            pltpu.VMEM((rows, kv_lora), F32),
        ],
    )
    return pl.pallas_call(
        kern,
        grid_spec=grid_spec,
        out_shape=jax.ShapeDtypeStruct((nseq, t_pad, MLA_HEADS * MLA_V_DIM), F32),
        compiler_params=_cparams(("parallel", "arbitrary")),
        name="mla_sample_attention",
    )(page_table, qs, kvn, wuv, *([cache_kv] * pp), *([cache_kr_t] * pp))


def _mlstm_kernel(bias_ref, q_ref, k_ref, v_ref, mo_ref, gcol_ref, grow_ref, gmh_ref, c0_ref, n0_ref, m0_ref,
                  h_ref, c_ref, n_ref, m_ref, c_sc, n_sc, m_sc, *, chunk, nchunks, valid, hps):
    DK, DV = MLSTM_QK_DIM, MLSTM_V_DIM
    hg = pl.program_id(1)
    c_sc[...] = c0_ref[0]
    n_sc[...] = n0_ref[0]
    m_sc[...] = m0_ref[0]
    L = chunk
    li = lax.broadcasted_iota(I32, (L, L), 0)
    si = lax.broadcasted_iota(I32, (L, L), 1)
    causal = si <= li
    col_ok = lax.broadcasted_iota(I32, (L, 1), 0) < valid
    row_ok = lax.broadcasted_iota(I32, (1, L), 1) < valid
    kscale = MLSTM_QK_DIM ** -0.5

    def head_step(c, r0, hh_):
        qk = slice(hh_ * DK, (hh_ + 1) * DK)
        vv = slice(hh_ * DV, (hh_ + 1) * DV)
        b_i, b_f = bias_ref[0, hg * hps + hh_], bias_ref[1, hg * hps + hh_]
        q = q_ref[pl.ds(r0, L), qk]
        ks = k_ref[pl.ds(r0, L), qk] * kscale
        v = v_ref[pl.ds(r0, L), vv].astype(BF16)
        gc = gcol_ref[pl.ds(r0, L), 2 * hh_:2 * hh_ + 2]
        gr = grow_ref[0, hh_, pl.ds(c, 1), :]
        ig_col = jnp.where(col_ok, gc[:, 0:1] + b_i, -jnp.inf)
        ig_row = jnp.where(row_ok, gr[:, 0:L] + b_i, -jnp.inf)
        fl_col = jnp.where(col_ok, jax.nn.log_sigmoid(gc[:, 1:2] + b_f), 0.0)
        fl_row = jnp.where(row_ok, jax.nn.log_sigmoid(gr[:, L:2 * L] + b_f), 0.0)
        b_col = jnp.sum(jnp.where(causal, fl_row, 0.0), axis=1, keepdims=True)
        b_row = jnp.sum(jnp.where(li <= si, fl_col, 0.0), axis=0, keepdims=True)
        m_prev = m_sc[hh_:hh_ + 1, 0:1]
        n_prev = n_sc[hh_:hh_ + 1, :]
        c_prev = c_sc[hh_]
        dmat = jnp.where(causal, b_col - b_row + ig_row, -jnp.inf)
        m_inter = b_col + m_prev
        m_t = jnp.maximum(m_inter, jnp.max(dmat, axis=1, keepdims=True))
        w_inter = jnp.exp(m_inter - m_t)
        qb = q.astype(BF16)
        s = _dot_nt(qb, ks.astype(BF16)) * jnp.exp(dmat - m_t)
        num = w_inter * _dot(qb, c_prev.astype(BF16)) + _dot(s.astype(BF16), v)
        den = w_inter * jnp.sum(q * n_prev, axis=1, keepdims=True) + jnp.sum(s, axis=1, keepdims=True)
        hh = num / jnp.maximum(jnp.abs(den), jnp.exp(-m_t))
        y = _rms(hh, gmh_ref[:, vv])
        h_ref[pl.ds(r0, L), vv] = (y * jax.nn.sigmoid(mo_ref[pl.ds(r0, L), vv])).astype(h_ref.dtype)
        b_last = jnp.sum(fl_row, axis=1, keepdims=True)
        w_end_col = b_last - b_col + ig_col
        w_end_row = b_last - b_row + ig_row
        m_new = jnp.maximum(b_last + m_prev, jnp.max(w_end_row, axis=1, keepdims=True))
        a_state = jnp.exp(b_last + m_prev - m_new)
        ka = ks * jnp.exp(w_end_col - m_new)
        c_sc[hh_] = a_state * c_prev + _dot_tn(ka.astype(BF16), v)
        n_sc[hh_:hh_ + 1, :] = a_state * n_prev + jnp.sum(ka, axis=0, keepdims=True)
        m_sc[hh_:hh_ + 1, :] = jnp.broadcast_to(m_new, (1, m_sc.shape[1]))

    def step(c, carry):
        r0 = pl.multiple_of(c * L, L)
        for hh_ in range(hps):
            head_step(c, r0, hh_)
        return carry

    lax.fori_loop(0, nchunks, step, 0)
    c_ref[0] = c_sc[...]
    n_ref[0] = n_sc[...]
    m_ref[0] = m_sc[...]


def _mlstm(z2, gates, bias, g_mh, c0, n0, m0, batch, t_rows, chunk, valid, hps, zo):
    H, DK, DV = MLSTM_HEADS, MLSTM_QK_DIM, MLSTM_V_DIM
    nchunks = t_rows // chunk
    ng = H // hps
    g6 = gates.reshape(batch, t_rows, 2, ng, hps)
    gcol = g6.transpose(0, 3, 1, 4, 2).reshape(batch * ng * t_rows, 2 * hps)
    grow = g6.reshape(batch, nchunks, chunk, 2, ng, hps).transpose(0, 4, 5, 1, 3, 2)
    grow = grow.reshape(batch * ng, hps, nchunks, 2 * chunk)
    c0r = c0.reshape(batch * ng, hps, DK, DV)
    n0r = n0.reshape(batch * ng, hps, DK)
    m0r = jnp.broadcast_to(m0.reshape(batch * ng, hps, 1).astype(F32), (batch * ng, hps, LANE))
    kern = functools.partial(_mlstm_kernel, chunk=chunk, nchunks=nchunks, valid=valid, hps=hps)
    qk_w, v_w = hps * DK, hps * DV
    grp3 = lambda b, g, bias: (b * ng + g, 0, 0)
    grp4 = lambda b, g, bias: (b * ng + g, 0, 0, 0)
    grid_spec = pltpu.PrefetchScalarGridSpec(
        num_scalar_prefetch=1,
        grid=(batch, ng),
        in_specs=[
            pl.BlockSpec((t_rows, qk_w), lambda b, g, bias: (b, zo["mq"] // qk_w + g)),
            pl.BlockSpec((t_rows, qk_w), lambda b, g, bias: (b, zo["mk"] // qk_w + g)),
            pl.BlockSpec((t_rows, v_w), lambda b, g, bias: (b, zo["mv"] // v_w + g)),
            pl.BlockSpec((t_rows, v_w), lambda b, g, bias: (b, zo["mo"] // v_w + g)),
            pl.BlockSpec((t_rows, 2 * hps), lambda b, g, bias: (b * ng + g, 0)),
            pl.BlockSpec((1, hps, nchunks, 2 * chunk), grp4),
            pl.BlockSpec((1, v_w), lambda b, g, bias: (0, g)),
            pl.BlockSpec((1, hps, DK, DV), grp4),
            pl.BlockSpec((1, hps, DK), grp3),
            pl.BlockSpec((1, hps, LANE), grp3),
        ],
        out_specs=[
            pl.BlockSpec((t_rows, v_w), lambda b, g, bias: (b, g)),
            pl.BlockSpec((1, hps, DK, DV), grp4),
            pl.BlockSpec((1, hps, DK), grp3),
            pl.BlockSpec((1, hps, LANE), grp3),
        ],
        scratch_shapes=[pltpu.VMEM((hps, DK, DV), F32), pltpu.VMEM((hps, DK), F32), pltpu.VMEM((hps, LANE), F32)],
    )
    hg, c, n, m = pl.pallas_call(
        kern,
        grid_spec=grid_spec,
        out_shape=[
            jax.ShapeDtypeStruct((batch * t_rows, H * DV), BF16),
            jax.ShapeDtypeStruct((batch * ng, hps, DK, DV), F32),
            jax.ShapeDtypeStruct((batch * ng, hps, DK), F32),
            jax.ShapeDtypeStruct((batch * ng, hps, LANE), F32),
        ],
        compiler_params=_cparams(("parallel", "parallel")),
        name="mlstm",
    )(bias, z2, z2, z2, z2, gcol, grow, g_mh.reshape(1, H * DV), c0r, n0r, m0r)
    return hg, c.reshape(batch, H, DK, DV), n.reshape(batch, H, DK), m[:, :, 0].reshape(batch, H)


def _take_top(work, order, nsel, want_rank):
    rank = jnp.full(work.shape, NOT_A_RANK, F32) if want_rank else None
    vals, picks = [], []
    for a in range(nsel):
        m = jnp.max(work, axis=0, keepdims=True)
        am = jnp.min(jnp.where(work == m, order, ORDER_MAX), axis=0, keepdims=True)
        hit = order == am
        vals.append(m)
        picks.append(am)
        if want_rank:
            rank = jnp.where(hit, float(a), rank)
        work = jnp.where(hit, -jnp.inf, work)
    return vals, picks, rank


def _candidate_blocks(tm):
    K = PEER_TOPK
    blocks, pos, valid = [], [], []
    for a in range(CAND_ROW_BLOCKS):
        nb = K // (a + 1)
        rows = SUBLANE * ((nb + SUBLANE - 1) // SUBLANE)
        it = lax.broadcasted_iota(I32, (rows, tm), 0)
        blocks.append(("row", a, rows))
        pos.append(a * K + it)
        valid.append(it < nb)
    for b in range(K):
        na = K // (b + 1)
        if na <= CAND_ROW_BLOCKS:
            break
        rows = SUBLANE * ((na + SUBLANE - 1) // SUBLANE)
        it = lax.broadcasted_iota(I32, (rows, tm), 0)
        blocks.append(("col", b, rows))
        pos.append(it * K + b)
        valid.append((it >= CAND_ROW_BLOCKS) & (it < na))
    return blocks, jnp.concatenate(pos, axis=0), jnp.concatenate(valid, axis=0)


def _peer_select_kernel(pq_ref, keys_ref, e1_ref, ks_ref, e2_ref, rb_ref):
    K = PEER_TOPK
    tm = pq_ref.shape[0]
    key_row = lax.broadcasted_iota(I32, (PEER_N_KEYS, tm), 0)
    blocks, cand_pos, cand_ok = _candidate_blocks(tm)
    a_row = lax.broadcasted_iota(I32, (K, tm), 0)
    for h in range(PEER_HEADS):
        sides = []
        for p in range(2):
            qs = pq_ref[:, (2 * h + p) * LANE:(2 * h + p + 1) * LANE].astype(BF16)
            st = _dot_nt(keys_ref[2 * h + p], qs)
            vals, _, rank = _take_top(st, key_row, K, True)
            sides.append((st, vals, rank))
        (s1, v1, ra), (s2, v2, rb) = sides
        v1all = jnp.concatenate(v1, axis=0)
        v2all = jnp.concatenate(v2, axis=0)
        parts = [(v1[i] + v2all[:rows]) if kind == "row" else (v1all[:rows] + v2[i]) for kind, i, rows in blocks]
        cand = jnp.where(cand_ok, jnp.concatenate(parts, axis=0), -jnp.inf)
        top, pos, _ = _take_top(cand, cand_pos, K, False)
        zsum = jnp.ones_like(top[0])
        for k in range(1, K):
            zsum = zsum + jnp.exp(top[k] - top[0])
        count = jnp.zeros((K, tm), F32)
        for k in range(K):
            count = count + jnp.where(lax.shift_right_logical(pos[k], 4) == a_row, 1.0, 0.0)
        last_b = jnp.full((PEER_N_KEYS, tm), -1.0, F32)
        for a in range(K):
            last_b = jnp.where(ra == float(a), count[a:a + 1] - 1.0, last_b)
        e1_ref[h] = jnp.where(ra < float(K), jnp.exp(s1 - v1[0]), 0.0) / zsum
        ks_ref[h] = last_b
        e2_ref[h] = jnp.where(rb < float(K), jnp.exp(s2 - v2[0]), 0.0)
        rb_ref[h] = rb


def _peer_select(pq, keys, tm):
    n = pq.shape[0]
    big = jax.ShapeDtypeStruct((PEER_HEADS, PEER_N_KEYS, n), F32)
    bspec = pl.BlockSpec((PEER_HEADS, PEER_N_KEYS, tm), lambda i: (0, 0, i))
    return pl.pallas_call(
        _peer_select_kernel,
        grid=(n // tm,),
        in_specs=[pl.BlockSpec((tm, pq.shape[1]), lambda i: (i, 0)), pl.BlockSpec(keys.shape, lambda i: (0, 0, 0))],
        out_specs=[bspec] * 4,
        out_shape=[big] * 4,
        compiler_params=_cparams(("parallel",)),
        name="peer_select",
    )(pq, keys)


def _peer_expert_kernel(xn_ref, wd_ref, wu_ref, e1_ref, ks_ref, e2_ref, rb_ref, o_ref, coef_sc, *, krows):
    i = pl.program_id(1)
    n_tiles = pl.num_programs(1) - 1

    @pl.when(i == 0)
    def _():
        o_ref[...] = jnp.zeros(o_ref.shape, F32)
        coef_sc[1] = jnp.zeros(coef_sc.shape[1:], BF16)

    row0 = lax.rem(jnp.minimum(i, n_tiles - 1), SUBLANE // krows) * krows
    gates = []
    for r in range(krows):
        acc = None
        for h in range(PEER_HEADS):
            first = pl.ds(row0 + r, 1)
            term = jnp.where(rb_ref[h] <= ks_ref[h, first, :], e2_ref[h], 0.0) * e1_ref[h, first, :]
            acc = term if acc is None else acc + term
        gates.append(acc)
    gate = jnp.concatenate(gates, axis=0)

    o_ref[...] += _dot_tn(coef_sc[lax.rem(i + 1, 2)], wu_ref[...])
    fold = gate[0:SUBLANE]
    for q in range(1, gate.shape[0] // SUBLANE):
        fold = fold + gate[q * SUBLANE:(q + 1) * SUBLANE]
    fold1 = fold[:, 0:LANE]
    for q in range(1, fold.shape[1] // LANE):
        fold1 = fold1 + fold[:, q * LANE:(q + 1) * LANE]
    bits = pltpu.bitcast(fold1, jnp.uint32)
    zero = lax.shift_right_logical(lax.shift_right_logical(bits, jnp.uint32(16)), jnp.uint32(16))
    o_ref[0:SUBLANE, 0:LANE] += pltpu.bitcast(zero, F32)

    a_t = _dot_nt(wd_ref[...], xn_ref[...])
    gel = 0.5 * a_t * (1.0 + lax.erf(a_t * INV_SQRT2))
    coef_sc[lax.rem(i, 2)] = (gate * gel).astype(BF16)


def _peer_experts(xn, wd, wu, sel, tm, te):
    n, d = xn.shape
    ne = wd.shape[0]
    krows = te // PEER_N_KEYS
    n_tiles = ne // te
    per_blk = SUBLANE // krows
    e1, ks, e2, rb = sel
    cur = lambda i: jnp.minimum(i, n_tiles - 1)
    spec1 = pl.BlockSpec((PEER_HEADS, SUBLANE, tm), lambda j, i: (0, cur(i) // per_blk, j))
    spec2 = pl.BlockSpec((PEER_HEADS, PEER_N_KEYS, tm), lambda j, i: (0, 0, j))
    return pl.pallas_call(
        functools.partial(_peer_expert_kernel, krows=krows),
        grid=(n // tm, n_tiles + 1),
        in_specs=[
            pl.BlockSpec((tm, d), lambda j, i: (j, 0)),
            pl.BlockSpec((te, d), lambda j, i: (cur(i), 0)),
            pl.BlockSpec((te, d), lambda j, i: (jnp.maximum(i - 1, 0), 0)),
            spec1, spec1, spec2, spec2,
        ],
        out_specs=pl.BlockSpec((tm, d), lambda j, i: (j, 0)),
        out_shape=jax.ShapeDtypeStruct((n, d), F32),
        scratch_shapes=[pltpu.VMEM((2, te, tm), BF16)],
        compiler_params=_cparams(("parallel", "arbitrary")),
        name="peer_experts",
    )(xn, wd, wu, e1, ks, e2, rb)


def _rope_table(pos):
    inv = ROPE_THETA ** (-jnp.arange(0, ROPE_DIM, 2, dtype=F32) / ROPE_DIM)
    ang = pos.astype(F32)[:, None] * inv[None, :]
    reps = LANE // (ROPE_DIM // 2)
    return jnp.concatenate([jnp.tile(jnp.cos(ang), (1, reps)), jnp.tile(jnp.sin(ang), (1, reps))], axis=1)


def _rot_cols(w):
    half = ROPE_DIM // 2
    return jnp.concatenate([-w[..., half:], w[..., :half]], axis=-1)


def _pad_cols(w, width):
    return jnp.pad(w, [(0, 0)] * (w.ndim - 1) + [(0, width - w.shape[-1])])


def kernel(x_prompt, x_sample, p_prompt, p_sample, cache_kv_latent, cache_k_rope, state_mlstm_C, state_mlstm_n, state_mlstm_m, page_table, g_mix_norm, w_in, g_q_latent, w_uq, g_kv_latent, w_uk, w_uv, b_igate, b_fgate, g_mlstm_head, w_out, g_ffn_norm, peer_w_query, peer_sub_keys, peer_w_down, peer_w_up, g_ple_norm, w_ple, w_ple_gate, g_final_norm):
    B, S, D = x_prompt.shape
    DB, T, _ = x_sample.shape
    depth = w_in.shape[0]
    q_lora, kv_lora = g_q_latent.shape[1], g_kv_latent.shape[1]
    H, DK, DV = MLSTM_HEADS, MLSTM_QK_DIM, MLSTM_V_DIM
    n_p, n_s = B * S, DB * T
    n = n_p + n_s
    n_past = page_table.shape[1] * PAGE_SIZE
    scale = (NOPE_DIM + ROPE_DIM) ** -0.5
    t_pad = SUBLANE * ((T + SUBLANE - 1) // SUBLANE)

    zo, off = {}, 0
    for name, width in (("mq", H * DK), ("mk", H * DK), ("mv", H * DV), ("mo", H * DV), ("cq", q_lora),
                        ("ckv", kv_lora), ("kr", 2 * LANE), ("gate", LANE)):
        zo[name] = off
        off += width
    zw = 1024 * ((off + 1023) // 1024)

    cs = jnp.concatenate([jnp.tile(_rope_table(jnp.arange(S, dtype=I32)), (B, 1)),
                          jnp.tile(_rope_table(n_past + jnp.arange(T, dtype=I32)), (DB, 1))], axis=0)
    tmq = _tile(math.gcd(S, n_s), 128, BF16_SUBLANE)
    chunk_p = math.gcd(S, MLSTM_CHUNK)

    cache_kr_t = jnp.swapaxes(cache_k_rope, 2, 3)
    x = jnp.concatenate([x_prompt.reshape(n_p, D), x_sample.reshape(n_s, D)], axis=0)
    outs = {k: [] for k in ("ckv_p", "kr_p", "ckv_s", "kr_s", "C_p", "n_p", "m_p", "C_s", "n_s", "m_s")}
    for l in range(depth):
        sizes = (q_lora, kv_lora, ROPE_DIM, H * DK, H * DK, H * DV, H * DV, H, H)
        parts, st = [], 0
        for sz in sizes:
            parts.append(w_in[l][:, st:st + sz])
            st += sz
        wcq, wckv, wkr, wmq, wmk, wmv, wmo, wmi, wmf = parts
        w_z = jnp.concatenate([wmq, wmk, wmv, wmo, wcq, wckv, _pad_cols(wkr, LANE), _pad_cols(_rot_cols(wkr), LANE),
                               wmi, wmf], axis=1)
        w_z = _pad_cols(w_z, zw).astype(BF16)
        wq3 = w_uq[l].reshape(q_lora, MLA_HEADS, NOPE_DIM + ROPE_DIM)
        wn = wq3[:, :, :NOPE_DIM].reshape(q_lora, MLA_HEADS * NOPE_DIM).astype(BF16)
        wr = _pad_cols(wq3[:, :, NOPE_DIM:], LANE).reshape(q_lora, MLA_HEADS * LANE).astype(BF16)
        wrr = _pad_cols(_rot_cols(wq3[:, :, NOPE_DIM:]), LANE).reshape(q_lora, MLA_HEADS * LANE).astype(BF16)
        wuk = w_uk[l].reshape(kv_lora, MLA_HEADS, NOPE_DIM).transpose(1, 2, 0).astype(BF16)
        wuv = w_uv[l].reshape(kv_lora, MLA_HEADS, MLA_V_DIM).transpose(1, 0, 2).astype(BF16)
        bias = jnp.stack([b_igate[l], b_fgate[l]]).astype(F32)

        hn = _rmsnorm(x, g_mix_norm[l], BF16)
        z = _matmul(hn, w_z)

        q4, ckv, kr, kvb = _qprep(z, cs, g_q_latent[l], g_kv_latent[l], wn, wr, wrr, wuk, tmq, zo)
        o_mla_p = _attn_prompt(q4, kvb, wuv, B, S, tmq, scale)
        nblk_s = n_s // tmq
        qs = q4[n_p // tmq:].transpose(1, 0, 2, 3).reshape(MLA_HEADS, DB, T, KV_W).transpose(1, 0, 2, 3)
        qs = jnp.pad(qs, ((0, 0), (0, 0), (0, t_pad - T), (0, 0))).reshape(DB, MLA_HEADS * t_pad, KV_W)
        kvn = jnp.pad(kvb[n_p:].reshape(DB, T, KV_W), ((0, 0), (0, t_pad - T), (0, 0)))
        o_mla_s = _attn_sample(page_table, qs, kvn, wuv, cache_kv_latent, cache_kr_t, l, scale, T)
        o_mla = jnp.concatenate([o_mla_p, o_mla_s[:, :T].reshape(n_s, -1).astype(BF16)], axis=0)

        gz = z[:, zo["gate"]:zo["gate"] + 2 * H]
        c0p = jnp.zeros((B, H, DK, DV), F32)
        n0p = jnp.zeros((B, H, DK), F32)
        m0p = jnp.full((B, H), M_EMPTY, F32)
        hg_p, C_p, nn_p, mm_p = _mlstm(z, gz[:n_p].reshape(B, S, 2, H), bias, g_mlstm_head[l], c0p, n0p, m0p,
                                       B, S, chunk_p, chunk_p, MLSTM_HEADS_PER_STEP_PROMPT, zo)
        z_s = jnp.pad(z[n_p:].reshape(DB, T, zw), ((0, 0), (0, t_pad - T), (0, 0))).reshape(DB * t_pad, zw)
        g_s = jnp.pad(gz[n_p:].reshape(DB, T, 2, H), ((0, 0), (0, t_pad - T), (0, 0), (0, 0)))
        hg_s, C_s, nn_s, mm_s = _mlstm(z_s, g_s, bias, g_mlstm_head[l], state_mlstm_C[l].astype(F32),
                                       state_mlstm_n[l].astype(F32), state_mlstm_m[l], DB, t_pad, t_pad, T,
                                       MLSTM_HEADS_PER_STEP_SAMPLE, zo)
        o_ml = jnp.concatenate([hg_p, hg_s.reshape(DB, t_pad, H * DV)[:, :T].reshape(n_s, H * DV)], axis=0)

        x1 = _outproj(o_mla, o_ml, w_out[l].astype(BF16), x)

        xn = _rmsnorm(x1, g_ffn_norm[l], BF16)
        pq = _matmul(xn, peer_w_query[l].astype(BF16))
        keys = peer_sub_keys[l].reshape(2 * PEER_HEADS, PEER_N_KEYS, -1).astype(BF16)
        tms = _tile(n, 256, LANE)
        sel = _peer_select(pq, keys, tms)
        tmx = _tile(n, 512, LANE)
        peer_out = _peer_experts(xn, peer_w_down[l].astype(BF16), peer_w_up[l].astype(BF16), sel, tmx, 2 * PEER_N_KEYS)

        x2, xg = _add_rmsnorm(x1, peer_out, g_ple_norm[l])
        pcat = jnp.concatenate([p_prompt[l].reshape(n_p, -1), p_sample[l].reshape(n_s, -1)], axis=0).astype(BF16)
        x = _ple(xg, w_ple_gate[l].astype(BF16), pcat, w_ple[l].astype(BF16), x2)

        outs["ckv_p"].append(ckv[:n_p].reshape(B, S, kv_lora))
        outs["kr_p"].append(kr[:n_p].reshape(B, S, ROPE_DIM))
        outs["ckv_s"].append(ckv[n_p:].reshape(DB, T, kv_lora))
        outs["kr_s"].append(kr[n_p:].reshape(DB, T, ROPE_DIM))
        outs["C_p"].append(C_p)
        outs["n_p"].append(nn_p)
        outs["m_p"].append(mm_p)
        outs["C_s"].append(C_s.astype(state_mlstm_C.dtype))
        outs["n_s"].append(nn_s.astype(state_mlstm_n.dtype))
        outs["m_s"].append(mm_s.astype(state_mlstm_m.dtype))

    y = _rmsnorm(x, g_final_norm, F32)
    st = {k: jnp.stack(v) for k, v in outs.items()}
    return (y[:n_p].reshape(B, S, D), y[n_p:].reshape(DB, T, D), st["ckv_p"], st["kr_p"], st["ckv_s"], st["kr_s"],
            st["C_p"], st["n_p"], st["m_p"], st["C_s"], st["n_s"], st["m_s"])
```

```python
import functools
import math

import jax
import jax.numpy as jnp
from jax import lax
from jax.experimental import pallas as pl
from jax.experimental.pallas import tpu as pltpu

F32, BF16, I32 = jnp.float32, jnp.bfloat16, jnp.int32

MLA_HEADS = 16
NOPE_DIM = 128
ROPE_DIM = 64
MLA_V_DIM = 128
ROPE_THETA = 10000.0
MLSTM_HEADS = 8
MLSTM_QK_DIM = 128
MLSTM_V_DIM = 256
MLSTM_CHUNK = 64
M_EMPTY = -1e30
PEER_HEADS = 8
PEER_N_KEYS = 128
PEER_TOPK = 16
PAGE_SIZE = 128
EPS = 1e-6
INV_SQRT2 = 0.7071067811865476

LANE = 128
SUBLANE = 8
BF16_SUBLANE = 16
VMEM_LIMIT_BYTES = 60 * 1024 * 1024

KV_W = 640
NOT_A_RANK = 1.0e9
ORDER_MAX = 2 ** 30
CAND_ROW_BLOCKS = 4
MLSTM_KERNEL_CHUNK = 128
PEER_TOKEN_TILE = 512
PEER_EXPERT_TILE = 512
MLSTM_HEADS_PER_STEP_PROMPT = 2
MLSTM_HEADS_PER_STEP_SAMPLE = 8


def _tile(n, cap, mult):
    best = None
    d = mult
    while d <= min(n, cap):
        if n % d == 0:
            best = d
        d += mult
    return n if best is None else best


def _cparams(sem):
    return pltpu.CompilerParams(dimension_semantics=sem, vmem_limit_bytes=VMEM_LIMIT_BYTES)


def _dot(a, b):
    return jnp.dot(a, b, preferred_element_type=F32)


def _dot_nt(a, b):
    return lax.dot_general(a, b, (((1,), (1,)), ((), ())), preferred_element_type=F32)


def _dot_tn(a, b):
    return lax.dot_general(a, b, (((0,), (0,)), ((), ())), preferred_element_type=F32)


def _rms(x, g):
    return x * lax.rsqrt(jnp.mean(x * x, axis=-1, keepdims=True) + EPS) * g


def _norm_kernel(x_ref, g_ref, o_ref):
    o_ref[...] = _rms(x_ref[...], g_ref[...]).astype(o_ref.dtype)


def _rmsnorm(x, g, out_dtype):
    n, d = x.shape
    tm = _tile(n, 256, BF16_SUBLANE)
    return pl.pallas_call(
        _norm_kernel,
        grid=(n // tm,),
        in_specs=[pl.BlockSpec((tm, d), lambda i: (i, 0)), pl.BlockSpec((1, d), lambda i: (0, 0))],
        out_specs=pl.BlockSpec((tm, d), lambda i: (i, 0)),
        out_shape=jax.ShapeDtypeStruct((n, d), out_dtype),
        compiler_params=_cparams(("parallel",)),
        name="rmsnorm",
    )(x, g.reshape(1, d))


def _add_norm_kernel(a_ref, b_ref, g_ref, s_ref, o_ref):
    s = a_ref[...] + b_ref[...]
    s_ref[...] = s
    o_ref[...] = _rms(s, g_ref[...]).astype(o_ref.dtype)


def _add_rmsnorm(a, b, g):
    n, d = a.shape
    tm = _tile(n, 128, BF16_SUBLANE)
    row = pl.BlockSpec((tm, d), lambda i: (i, 0))
    return pl.pallas_call(
        _add_norm_kernel,
        grid=(n // tm,),
        in_specs=[row, row, pl.BlockSpec((1, d), lambda i: (0, 0))],
        out_specs=[row, row],
        out_shape=[jax.ShapeDtypeStruct((n, d), F32), jax.ShapeDtypeStruct((n, d), BF16)],
        compiler_params=_cparams(("parallel",)),
        name="add_rmsnorm",
    )(a, b, g.reshape(1, d))


def _mm_kernel(a_ref, b_ref, o_ref):
    o_ref[...] = _dot(a_ref[...], b_ref[...]).astype(o_ref.dtype)


def _matmul(a, b, out_dtype=F32):
    m, k = a.shape
    n = b.shape[1]
    tm = _tile(m, 1088, BF16_SUBLANE)
    tn = _tile(n, 1024, LANE)
    return pl.pallas_call(
        _mm_kernel,
        grid=(m // tm, n // tn),
        in_specs=[pl.BlockSpec((tm, k), lambda i, j: (i, 0)), pl.BlockSpec((k, tn), lambda i, j: (0, j))],
        out_specs=pl.BlockSpec((tm, tn), lambda i, j: (i, j)),
        out_shape=jax.ShapeDtypeStruct((m, n), out_dtype),
        compiler_params=_cparams(("parallel", "parallel")),
        name="matmul",
    )(a, b)


def _outproj_kernel(a1_ref, a2_ref, b1_ref, b2_ref, r_ref, o_ref):
    o_ref[...] = r_ref[...] + (_dot(a1_ref[...], b1_ref[...]) + _dot(a2_ref[...], b2_ref[...]))


def _outproj(a1, a2, w, r):
    m, k1 = a1.shape
    k2 = a2.shape[1]
    n = w.shape[1]
    tm = _tile(m, 1088, BF16_SUBLANE)
    tn = _tile(n, 512, LANE)
    nb1 = k1 // k2
    return pl.pallas_call(
        _outproj_kernel,
        grid=(m // tm, n // tn),
        in_specs=[
            pl.BlockSpec((tm, k1), lambda i, j: (i, 0)),
            pl.BlockSpec((tm, k2), lambda i, j: (i, 0)),
            pl.BlockSpec((k1, tn), lambda i, j: (0, j)),
            pl.BlockSpec((k2, tn), lambda i, j: (nb1, j)),
            pl.BlockSpec((tm, tn), lambda i, j: (i, j)),
        ],
        out_specs=pl.BlockSpec((tm, tn), lambda i, j: (i, j)),
        out_shape=jax.ShapeDtypeStruct((m, n), F32),
        compiler_params=_cparams(("parallel", "parallel")),
        name="outproj",
    )(a1, a2, w, w, r)


def _ple_kernel(xg_ref, wg_ref, p_ref, wp_ref, x_ref, o_ref):
    gate = _dot(xg_ref[...], wg_ref[...])
    o_ref[...] = x_ref[...] + _dot(p_ref[...], wp_ref[...]) * jax.nn.sigmoid(gate)


def _ple(xg, w_gate, p, w_ple, x):
    m, k = xg.shape
    n = w_gate.shape[1]
    kp = p.shape[1]
    tm = _tile(m, 1088, BF16_SUBLANE)
    tn = _tile(n, 512, LANE)
    return pl.pallas_call(
        _ple_kernel,
        grid=(m // tm, n // tn),
        in_specs=[
            pl.BlockSpec((tm, k), lambda i, j: (i, 0)),
            pl.BlockSpec((k, tn), lambda i, j: (0, j)),
            pl.BlockSpec((tm, kp), lambda i, j: (i, 0)),
            pl.BlockSpec((kp, tn), lambda i, j: (0, j)),
            pl.BlockSpec((tm, tn), lambda i, j: (i, j)),
        ],
        out_specs=pl.BlockSpec((tm, tn), lambda i, j: (i, j)),
        out_shape=jax.ShapeDtypeStruct((m, n), F32),
        compiler_params=_cparams(("parallel", "parallel")),
        name="ple",
    )(xg, w_gate, p, w_ple, x)


def _qprep_kernel(cq_ref, ckv_ref, krk_ref, cs_ref, gq_ref, gkv_ref, wn_ref, wr_ref, wrr_ref, wuk_ref,
                  q_ref, ckv_out_ref, kr_out_ref, kv_ref):
    cqn = _rms(cq_ref[...], gq_ref[...]).astype(BF16)
    cs = cs_ref[...]
    cos1, sin1 = cs[:, :LANE], cs[:, LANE:]
    cos_t = jnp.concatenate([cos1] * MLA_HEADS, axis=1)
    sin_t = jnp.concatenate([sin1] * MLA_HEADS, axis=1)
    qn = _dot(cqn, wn_ref[...])
    qr = _dot(cqn, wr_ref[...]) * cos_t + _dot(cqn, wrr_ref[...]) * sin_t
    for h in range(MLA_HEADS):
        sl = slice(h * LANE, (h + 1) * LANE)
        qa = _dot(qn[:, sl].astype(BF16), wuk_ref[h])
        q_ref[0, h, :, 0:512] = qa.astype(BF16)
        q_ref[0, h, :, 512:KV_W] = qr[:, sl].astype(BF16)
    c = _rms(ckv_ref[...], gkv_ref[...])
    ckv_out_ref[...] = c
    krk = krk_ref[...]
    kr = krk[:, :LANE] * cos1 + krk[:, LANE:] * sin1
    kr_out_ref[...] = kr[:, :ROPE_DIM]
    kv_ref[:, 0:512] = c.astype(BF16)
    kv_ref[:, 512:KV_W] = kr.astype(BF16)


def _qprep(z, cs, g_q, g_kv, wn, wr, wrr, wuk, tm, zo):
    n = z.shape[0]
    q_lora = wn.shape[0]
    kv_lora = wuk.shape[2]
    const2 = lambda i: (0, 0)
    return pl.pallas_call(
        _qprep_kernel,
        grid=(n // tm,),
        in_specs=[
            pl.BlockSpec((tm, q_lora), lambda i: (i, zo["cq"] // q_lora)),
            pl.BlockSpec((tm, kv_lora), lambda i: (i, zo["ckv"] // kv_lora)),
            pl.BlockSpec((tm, 2 * LANE), lambda i: (i, zo["kr"] // (2 * LANE))),
            pl.BlockSpec((tm, 2 * LANE), lambda i: (i, 0)),
            pl.BlockSpec((1, q_lora), const2),
            pl.BlockSpec((1, kv_lora), const2),
            pl.BlockSpec(wn.shape, const2),
            pl.BlockSpec(wr.shape, const2),
            pl.BlockSpec(wrr.shape, const2),
            pl.BlockSpec(wuk.shape, lambda i: (0, 0, 0)),
        ],
        out_specs=[
            pl.BlockSpec((1, MLA_HEADS, tm, KV_W), lambda i: (i, 0, 0, 0)),
            pl.BlockSpec((tm, kv_lora), lambda i: (i, 0)),
            pl.BlockSpec((tm, ROPE_DIM), lambda i: (i, 0)),
            pl.BlockSpec((tm, KV_W), lambda i: (i, 0)),
        ],
        out_shape=[
            jax.ShapeDtypeStruct((n // tm, MLA_HEADS, tm, KV_W), BF16),
            jax.ShapeDtypeStruct((n, kv_lora), F32),
            jax.ShapeDtypeStruct((n, ROPE_DIM), F32),
            jax.ShapeDtypeStruct((n, KV_W), BF16),
        ],
        compiler_params=_cparams(("parallel",)),
        name="mla_qprep",
    )(z, z, z, cs, g_q.reshape(1, -1), g_kv.reshape(1, -1), wn, wr, wrr, wuk)


def _softmax_step(s, v, m_sc, l_sc, acc_sc):
    m_old = m_sc[...]
    m_new = jnp.maximum(m_old, jnp.max(s, axis=-1, keepdims=True))
    alpha = jnp.exp(m_old - m_new)
    p = jnp.exp(s - m_new)
    l_sc[...] = alpha * l_sc[...] + jnp.sum(p, axis=-1, keepdims=True)
    acc_sc[...] = alpha * acc_sc[...] + _dot(p.astype(BF16), v)
    m_sc[...] = m_new


def _attn_prompt_kernel(q_ref, kv_ref, wuv_ref, o_ref, m_sc, l_sc, acc_sc, *, tq, tk, scale, kv_lora):
    qi = pl.program_id(1)
    rows = MLA_HEADS * tq
    q = q_ref[0].reshape(rows, KV_W)
    m_sc[...] = jnp.full(m_sc.shape, -jnp.inf, F32)
    l_sc[...] = jnp.zeros(l_sc.shape, F32)
    acc_sc[...] = jnp.zeros(acc_sc.shape, F32)
    nfull = (qi * tq) // tk

    def block(j, masked):
        k0 = pl.multiple_of(j * tk, tk)
        kblk = kv_ref[pl.ds(k0, tk), :]
        s = _dot_nt(q, kblk) * scale
        if masked:
            qpos = qi * tq + lax.rem(lax.broadcasted_iota(I32, (rows, 1), 0), tq)
            kpos = k0 + lax.broadcasted_iota(I32, (1, tk), 1)
            s = jnp.where(kpos <= qpos, s, -jnp.inf)
        _softmax_step(s, kblk[:, :kv_lora], m_sc, l_sc, acc_sc)

    def body(j, carry):
        block(j, False)
        return carry

    lax.fori_loop(0, nfull, body, 0)
    block(nfull, True)
    o = acc_sc[...] / l_sc[...]
    for h in range(MLA_HEADS):
        oh = o[h * tq:(h + 1) * tq].astype(BF16)
        o_ref[:, h * MLA_V_DIM:(h + 1) * MLA_V_DIM] = _dot(oh, wuv_ref[h]).astype(o_ref.dtype)


def _attn_prompt(q4, kv, wuv, batch, seq, tq, scale):
    kv_lora = wuv.shape[1]
    nq = seq // tq
    tk = _tile(seq, 512, tq)
    rows = MLA_HEADS * tq
    kern = functools.partial(_attn_prompt_kernel, tq=tq, tk=tk, scale=scale, kv_lora=kv_lora)
    return pl.pallas_call(
        kern,
        grid=(batch, nq),
        in_specs=[
            pl.BlockSpec((1, MLA_HEADS, tq, KV_W), lambda b, i: (b * nq + i, 0, 0, 0)),
            pl.BlockSpec((seq, KV_W), lambda b, i: (b, 0)),
            pl.BlockSpec(wuv.shape, lambda b, i: (0, 0, 0)),
        ],
        out_specs=pl.BlockSpec((tq, MLA_HEADS * MLA_V_DIM), lambda b, i: (b * nq + i, 0)),
        out_shape=jax.ShapeDtypeStruct((batch * seq, MLA_HEADS * MLA_V_DIM), BF16),
        scratch_shapes=[pltpu.VMEM((rows, 1), F32), pltpu.VMEM((rows, 1), F32), pltpu.VMEM((rows, kv_lora), F32)],
        compiler_params=_cparams(("parallel", "arbitrary")),
        name="mla_prompt_attention",
    )(q4, kv, wuv)


def _page_copies(pt_ref, ckv_hbm, ckr_hbm, cin, rin, sem_c, sem_r, seq, chunk, slot, *, pp, layer):
    page = ckv_hbm.shape[2]
    copies = []
    for i in range(pp):
        pid = pt_ref[seq, chunk * pp + i]
        copies.append(pltpu.make_async_copy(ckv_hbm.at[layer, pid], cin.at[slot, pl.ds(i * page, page)], sem_c.at[slot]))
        copies.append(pltpu.make_async_copy(ckr_hbm.at[layer, pid], rin.at[slot, i], sem_r.at[slot]))
    return copies


def _attn_sample_kernel(pt_ref, q_ref, kvn_ref, wuv_ref, ckv_hbm, ckr_hbm, o_ref, cin, rin, sem_c, sem_r, cbuf, rbuf,
                        m_sc, l_sc, acc_sc, *, pp, layer, scale, t_valid, t_pad, kv_lora):
    s_id, c = pl.program_id(0), pl.program_id(1)
    nchunk = pl.num_programs(1)
    k = s_id * nchunk + c
    slot = lax.rem(k, 2)
    page = ckv_hbm.shape[2]
    copies = functools.partial(_page_copies, pt_ref, ckv_hbm, ckr_hbm, cin, rin, sem_c, sem_r, pp=pp, layer=layer)

    @pl.when(k == 0)
    def _():
        for j, cp in enumerate(copies(s_id, c, slot)):
            cp.start(priority=(j // 2) % 2)

    @pl.when(k + 1 < pl.num_programs(0) * nchunk)
    def _():
        wrap = c + 1 == nchunk
        nxt_s = jnp.where(wrap, s_id + 1, s_id)
        nxt_c = jnp.where(wrap, 0, c + 1)
        for j, cp in enumerate(copies(nxt_s, nxt_c, 1 - slot)):
            cp.start(priority=(j // 2) % 2)

    @pl.when(c == 0)
    def _():
        m_sc[...] = jnp.full(m_sc.shape, -jnp.inf, F32)
        l_sc[...] = jnp.zeros(l_sc.shape, F32)
        acc_sc[...] = jnp.zeros(acc_sc.shape, F32)

    for cp in copies(s_id, c, slot):
        cp.wait()
    for i in range(pp):
        cbuf[i * page:(i + 1) * page, :] = cin[slot, pl.ds(i * page, page), :].astype(BF16)
        rbuf[:, i * page:(i + 1) * page] = rin[slot, i].astype(BF16)
    q = q_ref[0]
    kc = cbuf[...]
    s = _dot_nt(q[:, :kv_lora], kc) + _dot(q[:, kv_lora:kv_lora + ROPE_DIM], rbuf[...])
    _softmax_step(s * scale, kc, m_sc, l_sc, acc_sc)

    @pl.when(c == pl.num_programs(1) - 1)
    def _():
        kn = kvn_ref[0]
        s = _dot_nt(q, kn) * scale
        qt = lax.rem(lax.broadcasted_iota(I32, (q.shape[0], 1), 0), t_pad)
        kt = lax.broadcasted_iota(I32, (1, t_pad), 1)
        s = jnp.where((kt <= qt) & (kt < t_valid), s, -jnp.inf)
        _softmax_step(s, kn[:, :kv_lora], m_sc, l_sc, acc_sc)
        o = acc_sc[...] / l_sc[...]
        for h in range(MLA_HEADS):
            oh = o[h * t_pad:(h + 1) * t_pad].astype(BF16)
            o_ref[0, :, h * MLA_V_DIM:(h + 1) * MLA_V_DIM] = _dot(oh, wuv_ref[h])


def _attn_sample(page_table, qs, kvn, wuv, cache_kv, cache_kr_t, layer, scale, t_valid):
    nseq, rows, _ = qs.shape
    t_pad = kvn.shape[1]
    npages = page_table.shape[1]
    page, kv_lora = cache_kv.shape[2], cache_kv.shape[3]
    pp = _tile(npages, 32, 1)
    kern = functools.partial(_attn_sample_kernel, pp=pp, layer=layer, scale=scale, t_valid=t_valid, t_pad=t_pad,
                             kv_lora=kv_lora)
    grid_spec = pltpu.PrefetchScalarGridSpec(
        num_scalar_prefetch=1,
        grid=(nseq, npages // pp),
        in_specs=[
            pl.BlockSpec((1, rows, KV_W), lambda s, c, pt: (s, 0, 0)),
            pl.BlockSpec((1, t_pad, KV_W), lambda s, c, pt: (s, 0, 0)),
            pl.BlockSpec(wuv.shape, lambda s, c, pt: (0, 0, 0)),
            pl.BlockSpec(memory_space=pl.ANY),
            pl.BlockSpec(memory_space=pl.ANY),
        ],
        out_specs=pl.BlockSpec((1, t_pad, MLA_HEADS * MLA_V_DIM), lambda s, c, pt: (s, 0, 0)),
        scratch_shapes=[
            pltpu.VMEM((2, pp * page, kv_lora), F32),
            pltpu.VMEM((2, pp, ROPE_DIM, page), F32),
            pltpu.SemaphoreType.DMA((2,)),
            pltpu.SemaphoreType.DMA((2,)),
            pltpu.VMEM((pp * page, kv_lora), BF16),
            pltpu.VMEM((ROPE_DIM, pp * page), BF16),
            pltpu.VMEM((rows, 1), F32),
            pltpu.VMEM((rows, 1), F32),
            pltpu.VMEM((rows, kv_lora), F32),
        ],
    )
    return pl.pallas_call(
        kern,
        grid_spec=grid_spec,
        out_shape=jax.ShapeDtypeStruct((nseq, t_pad, MLA_HEADS * MLA_V_DIM), F32),
        compiler_params=_cparams(("arbitrary", "arbitrary")),
        name="mla_sample_attention",
    )(page_table, qs, kvn, wuv, cache_kv, cache_kr_t)


def _mlstm_kernel(bias_ref, q_ref, k_ref, v_ref, mo_ref, gcol_ref, grow_ref, gmh_ref, c0_ref, n0_ref, m0_ref,
                  h_ref, c_ref, n_ref, m_ref, c_sc, n_sc, m_sc, *, chunk, nchunks, valid, hps):
    DK, DV = MLSTM_QK_DIM, MLSTM_V_DIM
    hg = pl.program_id(1)
    c_sc[...] = c0_ref[0]
    n_sc[...] = n0_ref[0]
    m_sc[...] = m0_ref[0]
    L = chunk
    li = lax.broadcasted_iota(I32, (L, L), 0)
    si = lax.broadcasted_iota(I32, (L, L), 1)
    causal = si <= li
    col_ok = lax.broadcasted_iota(I32, (L, 1), 0) < valid
    row_ok = lax.broadcasted_iota(I32, (1, L), 1) < valid
    kscale = MLSTM_QK_DIM ** -0.5

    def head_step(c, r0, hh_):
        qk = slice(hh_ * DK, (hh_ + 1) * DK)
        vv = slice(hh_ * DV, (hh_ + 1) * DV)
        b_i, b_f = bias_ref[0, hg * hps + hh_], bias_ref[1, hg * hps + hh_]
        q = q_ref[pl.ds(r0, L), qk]
        ks = k_ref[pl.ds(r0, L), qk] * kscale
        v = v_ref[pl.ds(r0, L), vv].astype(BF16)
        gc = gcol_ref[pl.ds(r0, L), 2 * hh_:2 * hh_ + 2]
        gr = grow_ref[0, hh_, pl.ds(c, 1), :]
        ig_col = jnp.where(col_ok, gc[:, 0:1] + b_i, -jnp.inf)
        ig_row = jnp.where(row_ok, gr[:, 0:L] + b_i, -jnp.inf)
        fl_col = jnp.where(col_ok, jax.nn.log_sigmoid(gc[:, 1:2] + b_f), 0.0)
        fl_row = jnp.where(row_ok, jax.nn.log_sigmoid(gr[:, L:2 * L] + b_f), 0.0)
        b_col = jnp.sum(jnp.where(causal, fl_row, 0.0), axis=1, keepdims=True)
        b_row = jnp.sum(jnp.where(li <= si, fl_col, 0.0), axis=0, keepdims=True)
        m_prev = m_sc[hh_:hh_ + 1, 0:1]
        n_prev = n_sc[hh_:hh_ + 1, :]
        c_prev = c_sc[hh_]
        dmat = jnp.where(causal, b_col - b_row + ig_row, -jnp.inf)
        m_inter = b_col + m_prev
        m_t = jnp.maximum(m_inter, jnp.max(dmat, axis=1, keepdims=True))
        w_inter = jnp.exp(m_inter - m_t)
        qb = q.astype(BF16)
        s = _dot_nt(qb, ks.astype(BF16)) * jnp.exp(dmat - m_t)
        num = w_inter * _dot(qb, c_prev.astype(BF16)) + _dot(s.astype(BF16), v)
        den = w_inter * jnp.sum(q * n_prev, axis=1, keepdims=True) + jnp.sum(s, axis=1, keepdims=True)
        hh = num / jnp.maximum(jnp.abs(den), jnp.exp(-m_t))
        y = _rms(hh, gmh_ref[:, vv])
        h_ref[pl.ds(r0, L), vv] = (y * jax.nn.sigmoid(mo_ref[pl.ds(r0, L), vv])).astype(h_ref.dtype)
        b_last = jnp.sum(fl_row, axis=1, keepdims=True)
        w_end_col = b_last - b_col + ig_col
        w_end_row = b_last - b_row + ig_row
        m_new = jnp.maximum(b_last + m_prev, jnp.max(w_end_row, axis=1, keepdims=True))
        a_state = jnp.exp(b_last + m_prev - m_new)
        ka = ks * jnp.exp(w_end_col - m_new)
        c_sc[hh_] = a_state * c_prev + _dot_tn(ka.astype(BF16), v)
        n_sc[hh_:hh_ + 1, :] = a_state * n_prev + jnp.sum(ka, axis=0, keepdims=True)
        m_sc[hh_:hh_ + 1, :] = jnp.broadcast_to(m_new, (1, m_sc.shape[1]))

    def step(c, carry):
        r0 = pl.multiple_of(c * L, L)
        for hh_ in range(hps):
            head_step(c, r0, hh_)
        return carry

    lax.fori_loop(0, nchunks, step, 0)
    c_ref[0] = c_sc[...]
    n_ref[0] = n_sc[...]
    m_ref[0] = m_sc[...]


def _mlstm(z2, gates, bias, g_mh, c0, n0, m0, batch, t_rows, chunk, valid, hps, zo):
    H, DK, DV = MLSTM_HEADS, MLSTM_QK_DIM, MLSTM_V_DIM
    nchunks = t_rows // chunk
    ng = H // hps
    g6 = gates.reshape(batch, t_rows, 2, ng, hps)
    gcol = g6.transpose(0, 3, 1, 4, 2).reshape(batch * ng * t_rows, 2 * hps)
    grow = g6.reshape(batch, nchunks, chunk, 2, ng, hps).transpose(0, 4, 5, 1, 3, 2)
    grow = grow.reshape(batch * ng, hps, nchunks, 2 * chunk)
    c0r = c0.reshape(batch * ng, hps, DK, DV)
    n0r = n0.reshape(batch * ng, hps, DK)
    m0r = jnp.broadcast_to(m0.reshape(batch * ng, hps, 1).astype(F32), (batch * ng, hps, LANE))
    kern = functools.partial(_mlstm_kernel, chunk=chunk, nchunks=nchunks, valid=valid, hps=hps)
    qk_w, v_w = hps * DK, hps * DV
    grp3 = lambda b, g, bias: (b * ng + g, 0, 0)
    grp4 = lambda b, g, bias: (b * ng + g, 0, 0, 0)
    grid_spec = pltpu.PrefetchScalarGridSpec(
        num_scalar_prefetch=1,
        grid=(batch, ng),
        in_specs=[
            pl.BlockSpec((t_rows, qk_w), lambda b, g, bias: (b, zo["mq"] // qk_w + g)),
            pl.BlockSpec((t_rows, qk_w), lambda b, g, bias: (b, zo["mk"] // qk_w + g)),
            pl.BlockSpec((t_rows, v_w), lambda b, g, bias: (b, zo["mv"] // v_w + g)),
            pl.BlockSpec((t_rows, v_w), lambda b, g, bias: (b, zo["mo"] // v_w + g)),
            pl.BlockSpec((t_rows, 2 * hps), lambda b, g, bias: (b * ng + g, 0)),
            pl.BlockSpec((1, hps, nchunks, 2 * chunk), grp4),
            pl.BlockSpec((1, v_w), lambda b, g, bias: (0, g)),
            pl.BlockSpec((1, hps, DK, DV), grp4),
            pl.BlockSpec((1, hps, DK), grp3),
            pl.BlockSpec((1, hps, LANE), grp3),
        ],
        out_specs=[
            pl.BlockSpec((t_rows, v_w), lambda b, g, bias: (b, g)),
            pl.BlockSpec((1, hps, DK, DV), grp4),
            pl.BlockSpec((1, hps, DK), grp3),
            pl.BlockSpec((1, hps, LANE), grp3),
        ],
        scratch_shapes=[pltpu.VMEM((hps, DK, DV), F32), pltpu.VMEM((hps, DK), F32), pltpu.VMEM((hps, LANE), F32)],
    )
    hg, c, n, m = pl.pallas_call(
        kern,
        grid_spec=grid_spec,
        out_shape=[
            jax.ShapeDtypeStruct((batch * t_rows, H * DV), BF16),
            jax.ShapeDtypeStruct((batch * ng, hps, DK, DV), F32),
            jax.ShapeDtypeStruct((batch * ng, hps, DK), F32),
            jax.ShapeDtypeStruct((batch * ng, hps, LANE), F32),
        ],
        compiler_params=_cparams(("parallel", "parallel")),
        name="mlstm",
    )(bias, z2, z2, z2, z2, gcol, grow, g_mh.reshape(1, H * DV), c0r, n0r, m0r)
    return hg, c.reshape(batch, H, DK, DV), n.reshape(batch, H, DK), m[:, :, 0].reshape(batch, H)


def _take_top(work, order, nsel, want_rank):
    rank = jnp.full(work.shape, NOT_A_RANK, F32) if want_rank else None
    vals, picks = [], []
    for a in range(nsel):
        m = jnp.max(work, axis=0, keepdims=True)
        am = jnp.min(jnp.where(work == m, order, ORDER_MAX), axis=0, keepdims=True)
        hit = order == am
        vals.append(m)
        picks.append(am)
        if want_rank:
            rank = jnp.where(hit, float(a), rank)
        work = jnp.where(hit, -jnp.inf, work)
    return vals, picks, rank


def _candidate_blocks(tm):
    K = PEER_TOPK
    blocks, pos, valid = [], [], []
    for a in range(CAND_ROW_BLOCKS):
        nb = K // (a + 1)
        rows = SUBLANE * ((nb + SUBLANE - 1) // SUBLANE)
        it = lax.broadcasted_iota(I32, (rows, tm), 0)
        blocks.append(("row", a, rows))
        pos.append(a * K + it)
        valid.append(it < nb)
    for b in range(K):
        na = K // (b + 1)
        if na <= CAND_ROW_BLOCKS:
            break
        rows = SUBLANE * ((na + SUBLANE - 1) // SUBLANE)
        it = lax.broadcasted_iota(I32, (rows, tm), 0)
        blocks.append(("col", b, rows))
        pos.append(it * K + b)
        valid.append((it >= CAND_ROW_BLOCKS) & (it < na))
    return blocks, jnp.concatenate(pos, axis=0), jnp.concatenate(valid, axis=0)


def _peer_select_kernel(pq_ref, keys_ref, e1_ref, ks_ref, e2_ref, rb_ref):
    K = PEER_TOPK
    tm = pq_ref.shape[0]
    key_row = lax.broadcasted_iota(I32, (PEER_N_KEYS, tm), 0)
    blocks, cand_pos, cand_ok = _candidate_blocks(tm)
    a_row = lax.broadcasted_iota(I32, (K, tm), 0)
    for h in range(PEER_HEADS):
        sides = []
        for p in range(2):
            qs = pq_ref[:, (2 * h + p) * LANE:(2 * h + p + 1) * LANE].astype(BF16)
            st = _dot_nt(keys_ref[2 * h + p], qs)
            vals, _, rank = _take_top(st, key_row, K, True)
            sides.append((st, vals, rank))
        (s1, v1, ra), (s2, v2, rb) = sides
        v1all = jnp.concatenate(v1, axis=0)
        v2all = jnp.concatenate(v2, axis=0)
        parts = [(v1[i] + v2all[:rows]) if kind == "row" else (v1all[:rows] + v2[i]) for kind, i, rows in blocks]
        cand = jnp.where(cand_ok, jnp.concatenate(parts, axis=0), -jnp.inf)
        top, pos, _ = _take_top(cand, cand_pos, K, False)
        zsum = jnp.ones_like(top[0])
        for k in range(1, K):
            zsum = zsum + jnp.exp(top[k] - top[0])
        count = jnp.zeros((K, tm), F32)
        for k in range(K):
            count = count + jnp.where(lax.shift_right_logical(pos[k], 4) == a_row, 1.0, 0.0)
        last_b = jnp.full((PEER_N_KEYS, tm), -1.0, F32)
        for a in range(K):
            last_b = jnp.where(ra == float(a), count[a:a + 1] - 1.0, last_b)
        e1_ref[h] = jnp.where(ra < float(K), jnp.exp(s1 - v1[0]), 0.0) / zsum
        ks_ref[h] = last_b
        e2_ref[h] = jnp.where(rb < float(K), jnp.exp(s2 - v2[0]), 0.0)
        rb_ref[h] = rb


def _peer_select(pq, keys, tm):
    n = pq.shape[0]
    big = jax.ShapeDtypeStruct((PEER_HEADS, PEER_N_KEYS, n), F32)
    bspec = pl.BlockSpec((PEER_HEADS, PEER_N_KEYS, tm), lambda i: (0, 0, i))
    return pl.pallas_call(
        _peer_select_kernel,
        grid=(n // tm,),
        in_specs=[pl.BlockSpec((tm, pq.shape[1]), lambda i: (i, 0)), pl.BlockSpec(keys.shape, lambda i: (0, 0, 0))],
        out_specs=[bspec] * 4,
        out_shape=[big] * 4,
        compiler_params=_cparams(("parallel",)),
        name="peer_select",
    )(pq, keys)


def _peer_expert_kernel(xn_ref, wd_ref, wu_ref, e1_ref, ks_ref, e2_ref, rb_ref, o_ref, coef_sc, *, krows):
    i = pl.program_id(1)
    n_tiles = pl.num_programs(1) - 1

    @pl.when(i == 0)
    def _():
        o_ref[...] = jnp.zeros(o_ref.shape, F32)
        coef_sc[1] = jnp.zeros(coef_sc.shape[1:], BF16)

    row0 = lax.rem(jnp.minimum(i, n_tiles - 1), SUBLANE // krows) * krows
    gates = []
    for r in range(krows):
        acc = None
        for h in range(PEER_HEADS):
            first = pl.ds(row0 + r, 1)
            term = jnp.where(rb_ref[h] <= ks_ref[h, first, :], e2_ref[h], 0.0) * e1_ref[h, first, :]
            acc = term if acc is None else acc + term
        gates.append(acc)
    gate = jnp.concatenate(gates, axis=0)

    o_ref[...] += _dot_tn(coef_sc[lax.rem(i + 1, 2)], wu_ref[...])
    fold = gate[0:SUBLANE]
    for q in range(1, gate.shape[0] // SUBLANE):
        fold = fold + gate[q * SUBLANE:(q + 1) * SUBLANE]
    fold1 = fold[:, 0:LANE]
    for q in range(1, fold.shape[1] // LANE):
        fold1 = fold1 + fold[:, q * LANE:(q + 1) * LANE]
    bits = pltpu.bitcast(fold1, jnp.uint32)
    zero = lax.shift_right_logical(lax.shift_right_logical(bits, jnp.uint32(16)), jnp.uint32(16))
    o_ref[0:SUBLANE, 0:LANE] += pltpu.bitcast(zero, F32)

    a_t = _dot_nt(wd_ref[...], xn_ref[...])
    gel = 0.5 * a_t * (1.0 + lax.erf(a_t * INV_SQRT2))
    coef_sc[lax.rem(i, 2)] = (gate * gel).astype(BF16)


def _peer_experts(xn, wd, wu, sel, tm, te):
    n, d = xn.shape
    ne = wd.shape[0]
    krows = te // PEER_N_KEYS
    n_tiles = ne // te
    per_blk = SUBLANE // krows
    e1, ks, e2, rb = sel
    cur = lambda i: jnp.minimum(i, n_tiles - 1)
    once = pl.Buffered(1)
    spec1 = pl.BlockSpec((PEER_HEADS, SUBLANE, tm), lambda j, i: (0, cur(i) // per_blk, j))
    spec2 = pl.BlockSpec((PEER_HEADS, PEER_N_KEYS, tm), lambda j, i: (0, 0, j), pipeline_mode=once)
    return pl.pallas_call(
        functools.partial(_peer_expert_kernel, krows=krows),
        grid=(pl.cdiv(n, tm), n_tiles + 1),
        in_specs=[
            pl.BlockSpec((tm, d), lambda j, i: (j, 0), pipeline_mode=once),
            pl.BlockSpec((te, d), lambda j, i: (cur(i), 0)),
            pl.BlockSpec((te, d), lambda j, i: (jnp.maximum(i - 1, 0), 0)),
            spec1, spec1, spec2, spec2,
        ],
        out_specs=pl.BlockSpec((tm, d), lambda j, i: (j, 0), pipeline_mode=once),
        out_shape=jax.ShapeDtypeStruct((n, d), F32),
        scratch_shapes=[pltpu.VMEM((2, te, tm), BF16)],
        compiler_params=_cparams(("parallel", "arbitrary")),
        name="peer_experts",
    )(xn, wd, wu, e1, ks, e2, rb)


def _rope_table(pos):
    inv = ROPE_THETA ** (-jnp.arange(0, ROPE_DIM, 2, dtype=F32) / ROPE_DIM)
    ang = pos.astype(F32)[:, None] * inv[None, :]
    reps = LANE // (ROPE_DIM // 2)
    return jnp.concatenate([jnp.tile(jnp.cos(ang), (1, reps)), jnp.tile(jnp.sin(ang), (1, reps))], axis=1)


def _rot_cols(w):
    half = ROPE_DIM // 2
    return jnp.concatenate([-w[..., half:], w[..., :half]], axis=-1)


def _pad_cols(w, width):
    return jnp.pad(w, [(0, 0)] * (w.ndim - 1) + [(0, width - w.shape[-1])])


def kernel(x_prompt, x_sample, p_prompt, p_sample, cache_kv_latent, cache_k_rope, state_mlstm_C, state_mlstm_n, state_mlstm_m, page_table, g_mix_norm, w_in, g_q_latent, w_uq, g_kv_latent, w_uk, w_uv, b_igate, b_fgate, g_mlstm_head, w_out, g_ffn_norm, peer_w_query, peer_sub_keys, peer_w_down, peer_w_up, g_ple_norm, w_ple, w_ple_gate, g_final_norm):
    B, S, D = x_prompt.shape
    DB, T, _ = x_sample.shape
    depth = w_in.shape[0]
    q_lora, kv_lora = g_q_latent.shape[1], g_kv_latent.shape[1]
    H, DK, DV = MLSTM_HEADS, MLSTM_QK_DIM, MLSTM_V_DIM
    n_p, n_s = B * S, DB * T
    n = n_p + n_s
    n_past = page_table.shape[1] * PAGE_SIZE
    scale = (NOPE_DIM + ROPE_DIM) ** -0.5
    t_pad = SUBLANE * ((T + SUBLANE - 1) // SUBLANE)

    zo, off = {}, 0
    for name, width in (("mq", H * DK), ("mk", H * DK), ("mv", H * DV), ("mo", H * DV), ("cq", q_lora),
                        ("ckv", kv_lora), ("kr", 2 * LANE), ("gate", LANE)):
        zo[name] = off
        off += width
    zw = 1024 * ((off + 1023) // 1024)

    cs = jnp.concatenate([jnp.tile(_rope_table(jnp.arange(S, dtype=I32)), (B, 1)),
                          jnp.tile(_rope_table(n_past + jnp.arange(T, dtype=I32)), (DB, 1))], axis=0)
    tmq = _tile(math.gcd(S, n_s), 128, BF16_SUBLANE)
    chunk_p = math.gcd(S, MLSTM_KERNEL_CHUNK)

    cache_kr_t = jnp.swapaxes(cache_k_rope, 2, 3)
    x = jnp.concatenate([x_prompt.reshape(n_p, D), x_sample.reshape(n_s, D)], axis=0)
    outs = {k: [] for k in ("ckv_p", "kr_p", "ckv_s", "kr_s", "C_p", "n_p", "m_p", "C_s", "n_s", "m_s")}
    for l in range(depth):
        sizes = (q_lora, kv_lora, ROPE_DIM, H * DK, H * DK, H * DV, H * DV, H, H)
        parts, st = [], 0
        for sz in sizes:
            parts.append(w_in[l][:, st:st + sz])
            st += sz
        wcq, wckv, wkr, wmq, wmk, wmv, wmo, wmi, wmf = parts
        w_z = jnp.concatenate([wmq, wmk, wmv, wmo, wcq, wckv, _pad_cols(wkr, LANE), _pad_cols(_rot_cols(wkr), LANE),
                               wmi, wmf], axis=1)
        w_z = _pad_cols(w_z, zw).astype(BF16)
        wq3 = w_uq[l].reshape(q_lora, MLA_HEADS, NOPE_DIM + ROPE_DIM)
        wn = wq3[:, :, :NOPE_DIM].reshape(q_lora, MLA_HEADS * NOPE_DIM).astype(BF16)
        wr = _pad_cols(wq3[:, :, NOPE_DIM:], LANE).reshape(q_lora, MLA_HEADS * LANE).astype(BF16)
        wrr = _pad_cols(_rot_cols(wq3[:, :, NOPE_DIM:]), LANE).reshape(q_lora, MLA_HEADS * LANE).astype(BF16)
        wuk = w_uk[l].reshape(kv_lora, MLA_HEADS, NOPE_DIM).transpose(1, 2, 0).astype(BF16)
        wuv = w_uv[l].reshape(kv_lora, MLA_HEADS, MLA_V_DIM).transpose(1, 0, 2).astype(BF16)
        bias = jnp.stack([b_igate[l], b_fgate[l]]).astype(F32)

        hn = _rmsnorm(x, g_mix_norm[l], BF16)
        z = _matmul(hn, w_z)

        q4, ckv, kr, kvb = _qprep(z, cs, g_q_latent[l], g_kv_latent[l], wn, wr, wrr, wuk, tmq, zo)
        o_mla_p = _attn_prompt(q4, kvb, wuv, B, S, tmq, scale)
        nblk_s = n_s // tmq
        qs = q4[n_p // tmq:].transpose(1, 0, 2, 3).reshape(MLA_HEADS, DB, T, KV_W).transpose(1, 0, 2, 3)
        qs = jnp.pad(qs, ((0, 0), (0, 0), (0, t_pad - T), (0, 0))).reshape(DB, MLA_HEADS * t_pad, KV_W)
        kvn = jnp.pad(kvb[n_p:].reshape(DB, T, KV_W), ((0, 0), (0, t_pad - T), (0, 0)))
        o_mla_s = _attn_sample(page_table, qs, kvn, wuv, cache_kv_latent, cache_kr_t, l, scale, T)
        o_mla = jnp.concatenate([o_mla_p, o_mla_s[:, :T].reshape(n_s, -1).astype(BF16)], axis=0)

        gz = z[:, zo["gate"]:zo["gate"] + 2 * H]
        c0p = jnp.zeros((B, H, DK, DV), F32)
        n0p = jnp.zeros((B, H, DK), F32)
        m0p = jnp.full((B, H), M_EMPTY, F32)
        hg_p, C_p, nn_p, mm_p = _mlstm(z, gz[:n_p].reshape(B, S, 2, H), bias, g_mlstm_head[l], c0p, n0p, m0p,
                                       B, S, chunk_p, chunk_p, MLSTM_HEADS_PER_STEP_PROMPT, zo)
        z_s = jnp.pad(z[n_p:].reshape(DB, T, zw), ((0, 0), (0, t_pad - T), (0, 0))).reshape(DB * t_pad, zw)
        g_s = jnp.pad(gz[n_p:].reshape(DB, T, 2, H), ((0, 0), (0, t_pad - T), (0, 0), (0, 0)))
        hg_s, C_s, nn_s, mm_s = _mlstm(z_s, g_s, bias, g_mlstm_head[l], state_mlstm_C[l].astype(F32),
                                       state_mlstm_n[l].astype(F32), state_mlstm_m[l], DB, t_pad, t_pad, T,
                                       MLSTM_HEADS_PER_STEP_SAMPLE, zo)
        o_ml = jnp.concatenate([hg_p, hg_s.reshape(DB, t_pad, H * DV)[:, :T].reshape(n_s, H * DV)], axis=0)

        x1 = _outproj(o_mla, o_ml, w_out[l].astype(BF16), x)

        xn = _rmsnorm(x1, g_ffn_norm[l], BF16)
        pq = _matmul(xn, peer_w_query[l].astype(BF16))
        keys = peer_sub_keys[l].reshape(2 * PEER_HEADS, PEER_N_KEYS, -1).astype(BF16)
        tms = _tile(n, 256, LANE)
        sel = _peer_select(pq, keys, tms)
        tmx = min(PEER_TOKEN_TILE, LANE * pl.cdiv(n, LANE))
        peer_out = _peer_experts(xn, peer_w_down[l].astype(BF16), peer_w_up[l].astype(BF16), sel, tmx, PEER_EXPERT_TILE)

        x2, xg = _add_rmsnorm(x1, peer_out, g_ple_norm[l])
        pcat = jnp.concatenate([p_prompt[l].reshape(n_p, -1), p_sample[l].reshape(n_s, -1)], axis=0).astype(BF16)
        x = _ple(xg, w_ple_gate[l].astype(BF16), pcat, w_ple[l].astype(BF16), x2)

        outs["ckv_p"].append(ckv[:n_p].reshape(B, S, kv_lora))
        outs["kr_p"].append(kr[:n_p].reshape(B, S, ROPE_DIM))
        outs["ckv_s"].append(ckv[n_p:].reshape(DB, T, kv_lora))
        outs["kr_s"].append(kr[n_p:].reshape(DB, T, ROPE_DIM))
        outs["C_p"].append(C_p)
        outs["n_p"].append(nn_p)
        outs["m_p"].append(mm_p)
        outs["C_s"].append(C_s.astype(state_mlstm_C.dtype))
        outs["n_s"].append(nn_s.astype(state_mlstm_n.dtype))
        outs["m_s"].append(mm_s.astype(state_mlstm_m.dtype))

    y = _rmsnorm(x, g_final_norm, F32)
    st = {k: jnp.stack(v) for k, v in outs.items()}
    return (y[:n_p].reshape(B, S, D), y[n_p:].reshape(DB, T, D), st["ckv_p"], st["kr_p"], st["ckv_s"], st["kr_s"],
            st["C_p"], st["n_p"], st["m_p"], st["C_s"], st["n_s"], st["m_s"])
```

```python
import functools
import math

import jax
import jax.numpy as jnp
from jax import lax
from jax.experimental import pallas as pl
from jax.experimental.pallas import tpu as pltpu

F32, BF16, I32 = jnp.float32, jnp.bfloat16, jnp.int32

MLA_HEADS = 16
NOPE_DIM = 128
ROPE_DIM = 64
MLA_V_DIM = 128
ROPE_THETA = 10000.0
MLSTM_HEADS = 8
MLSTM_QK_DIM = 128
MLSTM_V_DIM = 256
MLSTM_CHUNK = 64
M_EMPTY = -1e30
PEER_HEADS = 8
PEER_N_KEYS = 128
PEER_TOPK = 16
PAGE_SIZE = 128
EPS = 1e-6
INV_SQRT2 = 0.7071067811865476

LANE = 128
SUBLANE = 8
BF16_SUBLANE = 16
VMEM_LIMIT_BYTES = 60 * 1024 * 1024

KV_W = 640
NOT_A_RANK = 1.0e9
ORDER_MAX = 2 ** 30
CAND_ROW_BLOCKS = 4
MLSTM_KERNEL_CHUNK = 128
PEER_TOKEN_TILE = 512
PEER_EXPERT_TILE = 1024
MLSTM_HEADS_PER_STEP_PROMPT = 2
MLSTM_HEADS_PER_STEP_SAMPLE = 8


def _tile(n, cap, mult):
    best = None
    d = mult
    while d <= min(n, cap):
        if n % d == 0:
            best = d
        d += mult
    return n if best is None else best


def _cparams(sem):
    return pltpu.CompilerParams(dimension_semantics=sem, vmem_limit_bytes=VMEM_LIMIT_BYTES)


def _dot(a, b):
    return jnp.dot(a, b, preferred_element_type=F32)


def _dot_nt(a, b):
    return lax.dot_general(a, b, (((1,), (1,)), ((), ())), preferred_element_type=F32)


def _dot_tn(a, b):
    return lax.dot_general(a, b, (((0,), (0,)), ((), ())), preferred_element_type=F32)


def _rms(x, g):
    return x * lax.rsqrt(jnp.mean(x * x, axis=-1, keepdims=True) + EPS) * g


def _norm_kernel(x_ref, g_ref, o_ref):
    o_ref[...] = _rms(x_ref[...], g_ref[...]).astype(o_ref.dtype)


def _rmsnorm(x, g, out_dtype):
    n, d = x.shape
    tm = _tile(n, 256, BF16_SUBLANE)
    return pl.pallas_call(
        _norm_kernel,
        grid=(n // tm,),
        in_specs=[pl.BlockSpec((tm, d), lambda i: (i, 0)), pl.BlockSpec((1, d), lambda i: (0, 0))],
        out_specs=pl.BlockSpec((tm, d), lambda i: (i, 0)),
        out_shape=jax.ShapeDtypeStruct((n, d), out_dtype),
        compiler_params=_cparams(("parallel",)),
        name="rmsnorm",
    )(x, g.reshape(1, d))


def _add_norm_kernel(a_ref, b_ref, g_ref, s_ref, o_ref):
    s = a_ref[...] + b_ref[...]
    s_ref[...] = s
    o_ref[...] = _rms(s, g_ref[...]).astype(o_ref.dtype)


def _add_rmsnorm(a, b, g):
    n, d = a.shape
    tm = _tile(n, 128, BF16_SUBLANE)
    row = pl.BlockSpec((tm, d), lambda i: (i, 0))
    return pl.pallas_call(
        _add_norm_kernel,
        grid=(n // tm,),
        in_specs=[row, row, pl.BlockSpec((1, d), lambda i: (0, 0))],
        out_specs=[row, row],
        out_shape=[jax.ShapeDtypeStruct((n, d), F32), jax.ShapeDtypeStruct((n, d), BF16)],
        compiler_params=_cparams(("parallel",)),
        name="add_rmsnorm",
    )(a, b, g.reshape(1, d))


def _mm_kernel(a_ref, b_ref, o_ref):
    o_ref[...] = _dot(a_ref[...], b_ref[...]).astype(o_ref.dtype)


def _matmul(a, b, out_dtype=F32):
    m, k = a.shape
    n = b.shape[1]
    tm = _tile(m, 1088, BF16_SUBLANE)
    tn = _tile(n, 1024, LANE)
    return pl.pallas_call(
        _mm_kernel,
        grid=(m // tm, n // tn),
        in_specs=[pl.BlockSpec((tm, k), lambda i, j: (i, 0)), pl.BlockSpec((k, tn), lambda i, j: (0, j))],
        out_specs=pl.BlockSpec((tm, tn), lambda i, j: (i, j)),
        out_shape=jax.ShapeDtypeStruct((m, n), out_dtype),
        compiler_params=_cparams(("parallel", "parallel")),
        name="matmul",
    )(a, b)


def _outproj_kernel(a1_ref, a2_ref, b1_ref, b2_ref, r_ref, o_ref):
    o_ref[...] = r_ref[...] + (_dot(a1_ref[...], b1_ref[...]) + _dot(a2_ref[...], b2_ref[...]))


def _outproj(a1, a2, w, r):
    m, k1 = a1.shape
    k2 = a2.shape[1]
    n = w.shape[1]
    tm = _tile(m, 1088, BF16_SUBLANE)
    tn = _tile(n, 512, LANE)
    nb1 = k1 // k2
    return pl.pallas_call(
        _outproj_kernel,
        grid=(m // tm, n // tn),
        in_specs=[
            pl.BlockSpec((tm, k1), lambda i, j: (i, 0)),
            pl.BlockSpec((tm, k2), lambda i, j: (i, 0)),
            pl.BlockSpec((k1, tn), lambda i, j: (0, j)),
            pl.BlockSpec((k2, tn), lambda i, j: (nb1, j)),
            pl.BlockSpec((tm, tn), lambda i, j: (i, j)),
        ],
        out_specs=pl.BlockSpec((tm, tn), lambda i, j: (i, j)),
        out_shape=jax.ShapeDtypeStruct((m, n), F32),
        compiler_params=_cparams(("parallel", "parallel")),
        name="outproj",
    )(a1, a2, w, w, r)


def _ple_kernel(xg_ref, wg_ref, p_ref, wp_ref, x_ref, o_ref):
    gate = _dot(xg_ref[...], wg_ref[...])
    o_ref[...] = x_ref[...] + _dot(p_ref[...], wp_ref[...]) * jax.nn.sigmoid(gate)


def _ple(xg, w_gate, p, w_ple, x):
    m, k = xg.shape
    n = w_gate.shape[1]
    kp = p.shape[1]
    tm = _tile(m, 1088, BF16_SUBLANE)
    tn = _tile(n, 512, LANE)
    return pl.pallas_call(
        _ple_kernel,
        grid=(m // tm, n // tn),
        in_specs=[
            pl.BlockSpec((tm, k), lambda i, j: (i, 0)),
            pl.BlockSpec((k, tn), lambda i, j: (0, j)),
            pl.BlockSpec((tm, kp), lambda i, j: (i, 0)),
            pl.BlockSpec((kp, tn), lambda i, j: (0, j)),
            pl.BlockSpec((tm, tn), lambda i, j: (i, j)),
        ],
        out_specs=pl.BlockSpec((tm, tn), lambda i, j: (i, j)),
        out_shape=jax.ShapeDtypeStruct((m, n), F32),
        compiler_params=_cparams(("parallel", "parallel")),
        name="ple",
    )(xg, w_gate, p, w_ple, x)


def _qprep_kernel(cq_ref, ckv_ref, krk_ref, cs_ref, gq_ref, gkv_ref, wn_ref, wr_ref, wrr_ref, wuk_ref,
                  q_ref, ckv_out_ref, kr_out_ref, kv_ref):
    cqn = _rms(cq_ref[...], gq_ref[...]).astype(BF16)
    cs = cs_ref[...]
    cos1, sin1 = cs[:, :LANE], cs[:, LANE:]
    cos_t = jnp.concatenate([cos1] * MLA_HEADS, axis=1)
    sin_t = jnp.concatenate([sin1] * MLA_HEADS, axis=1)
    qn = _dot(cqn, wn_ref[...])
    qr = _dot(cqn, wr_ref[...]) * cos_t + _dot(cqn, wrr_ref[...]) * sin_t
    for h in range(MLA_HEADS):
        sl = slice(h * LANE, (h + 1) * LANE)
        qa = _dot(qn[:, sl].astype(BF16), wuk_ref[h])
        q_ref[0, h, :, 0:512] = qa.astype(BF16)
        q_ref[0, h, :, 512:KV_W] = qr[:, sl].astype(BF16)
    c = _rms(ckv_ref[...], gkv_ref[...])
    ckv_out_ref[...] = c
    krk = krk_ref[...]
    kr = krk[:, :LANE] * cos1 + krk[:, LANE:] * sin1
    kr_out_ref[...] = kr[:, :ROPE_DIM]
    kv_ref[:, 0:512] = c.astype(BF16)
    kv_ref[:, 512:KV_W] = kr.astype(BF16)


def _qprep(z, cs, g_q, g_kv, wn, wr, wrr, wuk, tm, zo):
    n = z.shape[0]
    q_lora = wn.shape[0]
    kv_lora = wuk.shape[2]
    const2 = lambda i: (0, 0)
    return pl.pallas_call(
        _qprep_kernel,
        grid=(n // tm,),
        in_specs=[
            pl.BlockSpec((tm, q_lora), lambda i: (i, zo["cq"] // q_lora)),
            pl.BlockSpec((tm, kv_lora), lambda i: (i, zo["ckv"] // kv_lora)),
            pl.BlockSpec((tm, 2 * LANE), lambda i: (i, zo["kr"] // (2 * LANE))),
            pl.BlockSpec((tm, 2 * LANE), lambda i: (i, 0)),
            pl.BlockSpec((1, q_lora), const2),
            pl.BlockSpec((1, kv_lora), const2),
            pl.BlockSpec(wn.shape, const2),
            pl.BlockSpec(wr.shape, const2),
            pl.BlockSpec(wrr.shape, const2),
            pl.BlockSpec(wuk.shape, lambda i: (0, 0, 0)),
        ],
        out_specs=[
            pl.BlockSpec((1, MLA_HEADS, tm, KV_W), lambda i: (i, 0, 0, 0)),
            pl.BlockSpec((tm, kv_lora), lambda i: (i, 0)),
            pl.BlockSpec((tm, ROPE_DIM), lambda i: (i, 0)),
            pl.BlockSpec((tm, KV_W), lambda i: (i, 0)),
        ],
        out_shape=[
            jax.ShapeDtypeStruct((n // tm, MLA_HEADS, tm, KV_W), BF16),
            jax.ShapeDtypeStruct((n, kv_lora), F32),
            jax.ShapeDtypeStruct((n, ROPE_DIM), F32),
            jax.ShapeDtypeStruct((n, KV_W), BF16),
        ],
        compiler_params=_cparams(("parallel",)),
        name="mla_qprep",
    )(z, z, z, cs, g_q.reshape(1, -1), g_kv.reshape(1, -1), wn, wr, wrr, wuk)


def _softmax_step(s, v, m_sc, l_sc, acc_sc):
    m_old = m_sc[...]
    m_new = jnp.maximum(m_old, jnp.max(s, axis=-1, keepdims=True))
    alpha = jnp.exp(m_old - m_new)
    p = jnp.exp(s - m_new)
    l_sc[...] = alpha * l_sc[...] + jnp.sum(p, axis=-1, keepdims=True)
    acc_sc[...] = alpha * acc_sc[...] + _dot(p.astype(BF16), v)
    m_sc[...] = m_new


def _attn_prompt_kernel(q_ref, kv_ref, wuv_ref, o_ref, m_sc, l_sc, acc_sc, s_sc, *, tq, tk, scale, kv_lora):
    qi = pl.program_id(1)
    rows = MLA_HEADS * tq
    q = q_ref[0].reshape(rows, KV_W)
    m_sc[...] = jnp.full(m_sc.shape, -jnp.inf, F32)
    l_sc[...] = jnp.zeros(l_sc.shape, F32)
    acc_sc[...] = jnp.zeros(acc_sc.shape, F32)
    nfull = (qi * tq) // tk

    def keys(j):
        return kv_ref[pl.ds(pl.multiple_of(j * tk, tk), tk), :]

    s_sc[0] = _dot_nt(q, keys(0)) * scale

    def body(j, carry):
        s = s_sc[lax.rem(j, 2)]
        s_sc[lax.rem(j + 1, 2)] = _dot_nt(q, keys(j + 1)) * scale
        _softmax_step(s, keys(j)[:, :kv_lora], m_sc, l_sc, acc_sc)
        return carry

    lax.fori_loop(0, nfull, body, 0)
    qpos = qi * tq + lax.rem(lax.broadcasted_iota(I32, (rows, 1), 0), tq)
    kpos = nfull * tk + lax.broadcasted_iota(I32, (1, tk), 1)
    s = jnp.where(kpos <= qpos, s_sc[lax.rem(nfull, 2)], -jnp.inf)
    _softmax_step(s, keys(nfull)[:, :kv_lora], m_sc, l_sc, acc_sc)
    o = acc_sc[...] / l_sc[...]
    for h in range(MLA_HEADS):
        oh = o[h * tq:(h + 1) * tq].astype(BF16)
        o_ref[:, h * MLA_V_DIM:(h + 1) * MLA_V_DIM] = _dot(oh, wuv_ref[h]).astype(o_ref.dtype)


def _attn_prompt(q4, kv, wuv, batch, seq, tq, scale):
    kv_lora = wuv.shape[1]
    nq = seq // tq
    tk = _tile(seq, 512, tq)
    rows = MLA_HEADS * tq
    kern = functools.partial(_attn_prompt_kernel, tq=tq, tk=tk, scale=scale, kv_lora=kv_lora)
    return pl.pallas_call(
        kern,
        grid=(batch, nq),
        in_specs=[
            pl.BlockSpec((1, MLA_HEADS, tq, KV_W), lambda b, i: (b * nq + i, 0, 0, 0)),
            pl.BlockSpec((seq, KV_W), lambda b, i: (b, 0)),
            pl.BlockSpec(wuv.shape, lambda b, i: (0, 0, 0)),
        ],
        out_specs=pl.BlockSpec((tq, MLA_HEADS * MLA_V_DIM), lambda b, i: (b * nq + i, 0)),
        out_shape=jax.ShapeDtypeStruct((batch * seq, MLA_HEADS * MLA_V_DIM), BF16),
        scratch_shapes=[pltpu.VMEM((rows, 1), F32), pltpu.VMEM((rows, 1), F32), pltpu.VMEM((rows, kv_lora), F32),
                        pltpu.VMEM((2, rows, tk), F32)],
        compiler_params=_cparams(("parallel", "arbitrary")),
        name="mla_prompt_attention",
    )(q4, kv, wuv)


def _page_copies(pt_ref, ckv_hbm, ckr_hbm, cin, rin, sem_c, sem_r, seq, chunk, slot, *, pp, layer):
    page = ckv_hbm.shape[2]
    copies = []
    for i in range(pp):
        pid = pt_ref[seq, chunk * pp + i]
        copies.append(pltpu.make_async_copy(ckv_hbm.at[layer, pid], cin.at[slot, pl.ds(i * page, page)], sem_c.at[slot]))
        copies.append(pltpu.make_async_copy(ckr_hbm.at[layer, pid], rin.at[slot, i], sem_r.at[slot]))
    return copies


def _attn_sample_kernel(pt_ref, q_ref, kvn_ref, wuv_ref, ckv_hbm, ckr_hbm, o_ref, cin, rin, sem_c, sem_r, cbuf, rbuf,
                        m_sc, l_sc, acc_sc, *, pp, layer, scale, t_valid, t_pad, kv_lora):
    s_id, c = pl.program_id(0), pl.program_id(1)
    nchunk = pl.num_programs(1)
    k = s_id * nchunk + c
    slot = lax.rem(k, 2)
    page = ckv_hbm.shape[2]
    copies = functools.partial(_page_copies, pt_ref, ckv_hbm, ckr_hbm, cin, rin, sem_c, sem_r, pp=pp, layer=layer)

    @pl.when(k == 0)
    def _():
        for j, cp in enumerate(copies(s_id, c, slot)):
            cp.start(priority=(j // 2) % 2)

    @pl.when(k + 1 < pl.num_programs(0) * nchunk)
    def _():
        wrap = c + 1 == nchunk
        nxt_s = jnp.where(wrap, s_id + 1, s_id)
        nxt_c = jnp.where(wrap, 0, c + 1)
        for j, cp in enumerate(copies(nxt_s, nxt_c, 1 - slot)):
            cp.start(priority=(j // 2) % 2)

    @pl.when(c == 0)
    def _():
        m_sc[...] = jnp.full(m_sc.shape, -jnp.inf, F32)
        l_sc[...] = jnp.zeros(l_sc.shape, F32)
        acc_sc[...] = jnp.zeros(acc_sc.shape, F32)

    for cp in copies(s_id, c, slot):
        cp.wait()
    for i in range(pp):
        cbuf[i * page:(i + 1) * page, :] = cin[slot, pl.ds(i * page, page), :].astype(BF16)
        rbuf[:, i * page:(i + 1) * page] = rin[slot, i].astype(BF16)
    q = q_ref[0]
    kc = cbuf[...]
    s = _dot_nt(q[:, :kv_lora], kc) + _dot(q[:, kv_lora:kv_lora + ROPE_DIM], rbuf[...])
    _softmax_step(s * scale, kc, m_sc, l_sc, acc_sc)

    @pl.when(c == pl.num_programs(1) - 1)
    def _():
        kn = kvn_ref[0]
        s = _dot_nt(q, kn) * scale
        qt = lax.rem(lax.broadcasted_iota(I32, (q.shape[0], 1), 0), t_pad)
        kt = lax.broadcasted_iota(I32, (1, t_pad), 1)
        s = jnp.where((kt <= qt) & (kt < t_valid), s, -jnp.inf)
        _softmax_step(s, kn[:, :kv_lora], m_sc, l_sc, acc_sc)
        o = acc_sc[...] / l_sc[...]
        for h in range(MLA_HEADS):
            oh = o[h * t_pad:(h + 1) * t_pad].astype(BF16)
            o_ref[0, :, h * MLA_V_DIM:(h + 1) * MLA_V_DIM] = _dot(oh, wuv_ref[h])


def _attn_sample(page_table, qs, kvn, wuv, cache_kv, cache_kr_t, layer, scale, t_valid):
    nseq, rows, _ = qs.shape
    t_pad = kvn.shape[1]
    npages = page_table.shape[1]
    page, kv_lora = cache_kv.shape[2], cache_kv.shape[3]
    pp = _tile(npages, 32, 1)
    kern = functools.partial(_attn_sample_kernel, pp=pp, layer=layer, scale=scale, t_valid=t_valid, t_pad=t_pad,
                             kv_lora=kv_lora)
    grid_spec = pltpu.PrefetchScalarGridSpec(
        num_scalar_prefetch=1,
        grid=(nseq, npages // pp),
        in_specs=[
            pl.BlockSpec((1, rows, KV_W), lambda s, c, pt: (s, 0, 0)),
            pl.BlockSpec((1, t_pad, KV_W), lambda s, c, pt: (s, 0, 0)),
            pl.BlockSpec(wuv.shape, lambda s, c, pt: (0, 0, 0)),
            pl.BlockSpec(memory_space=pl.ANY),
            pl.BlockSpec(memory_space=pl.ANY),
        ],
        out_specs=pl.BlockSpec((1, t_pad, MLA_HEADS * MLA_V_DIM), lambda s, c, pt: (s, 0, 0)),
        scratch_shapes=[
            pltpu.VMEM((2, pp * page, kv_lora), F32),
            pltpu.VMEM((2, pp, ROPE_DIM, page), F32),
            pltpu.SemaphoreType.DMA((2,)),
            pltpu.SemaphoreType.DMA((2,)),
            pltpu.VMEM((pp * page, kv_lora), BF16),
            pltpu.VMEM((ROPE_DIM, pp * page), BF16),
            pltpu.VMEM((rows, 1), F32),
            pltpu.VMEM((rows, 1), F32),
            pltpu.VMEM((rows, kv_lora), F32),
        ],
    )
    return pl.pallas_call(
        kern,
        grid_spec=grid_spec,
        out_shape=jax.ShapeDtypeStruct((nseq, t_pad, MLA_HEADS * MLA_V_DIM), F32),
        compiler_params=_cparams(("arbitrary", "arbitrary")),
        name="mla_sample_attention",
    )(page_table, qs, kvn, wuv, cache_kv, cache_kr_t)


def _mlstm_kernel(bias_ref, q_ref, k_ref, v_ref, mo_ref, gcol_ref, grow_ref, gmh_ref, c0_ref, n0_ref, m0_ref,
                  h_ref, c_ref, n_ref, m_ref, c_sc, n_sc, m_sc, *, chunk, nchunks, valid, hps):
    DK, DV = MLSTM_QK_DIM, MLSTM_V_DIM
    hg = pl.program_id(1)
    c_sc[...] = c0_ref[0]
    n_sc[...] = n0_ref[0]
    m_sc[...] = m0_ref[0]
    L = chunk
    li = lax.broadcasted_iota(I32, (L, L), 0)
    si = lax.broadcasted_iota(I32, (L, L), 1)
    causal = si <= li
    col_ok = lax.broadcasted_iota(I32, (L, 1), 0) < valid
    row_ok = lax.broadcasted_iota(I32, (1, L), 1) < valid
    kscale = MLSTM_QK_DIM ** -0.5

    def head_step(c, r0, hh_):
        qk = slice(hh_ * DK, (hh_ + 1) * DK)
        vv = slice(hh_ * DV, (hh_ + 1) * DV)
        b_i, b_f = bias_ref[0, hg * hps + hh_], bias_ref[1, hg * hps + hh_]
        q = q_ref[pl.ds(r0, L), qk]
        ks = k_ref[pl.ds(r0, L), qk] * kscale
        v = v_ref[pl.ds(r0, L), vv].astype(BF16)
        gc = gcol_ref[pl.ds(r0, L), 2 * hh_:2 * hh_ + 2]
        gr = grow_ref[0, hh_, pl.ds(c, 1), :]
        ig_col = jnp.where(col_ok, gc[:, 0:1] + b_i, -jnp.inf)
        ig_row = jnp.where(row_ok, gr[:, 0:L] + b_i, -jnp.inf)
        fl_col = jnp.where(col_ok, jax.nn.log_sigmoid(gc[:, 1:2] + b_f), 0.0)
        fl_row = jnp.where(row_ok, jax.nn.log_sigmoid(gr[:, L:2 * L] + b_f), 0.0)
        b_col = jnp.sum(jnp.where(causal, fl_row, 0.0), axis=1, keepdims=True)
        b_row = jnp.sum(jnp.where(li <= si, fl_col, 0.0), axis=0, keepdims=True)
        m_prev = m_sc[hh_:hh_ + 1, 0:1]
        n_prev = n_sc[hh_:hh_ + 1, :]
        c_prev = c_sc[hh_]
        dmat = jnp.where(causal, b_col - b_row + ig_row, -jnp.inf)
        m_inter = b_col + m_prev
        m_t = jnp.maximum(m_inter, jnp.max(dmat, axis=1, keepdims=True))
        w_inter = jnp.exp(m_inter - m_t)
        qb = q.astype(BF16)
        s = _dot_nt(qb, ks.astype(BF16)) * jnp.exp(dmat - m_t)
        num = w_inter * _dot(qb, c_prev.astype(BF16)) + _dot(s.astype(BF16), v)
        den = w_inter * jnp.sum(q * n_prev, axis=1, keepdims=True) + jnp.sum(s, axis=1, keepdims=True)
        hh = num / jnp.maximum(jnp.abs(den), jnp.exp(-m_t))
        y = _rms(hh, gmh_ref[:, vv])
        h_ref[pl.ds(r0, L), vv] = (y * jax.nn.sigmoid(mo_ref[pl.ds(r0, L), vv])).astype(h_ref.dtype)
        b_last = jnp.sum(fl_row, axis=1, keepdims=True)
        w_end_col = b_last - b_col + ig_col
        w_end_row = b_last - b_row + ig_row
        m_new = jnp.maximum(b_last + m_prev, jnp.max(w_end_row, axis=1, keepdims=True))
        a_state = jnp.exp(b_last + m_prev - m_new)
        ka = ks * jnp.exp(w_end_col - m_new)
        c_sc[hh_] = a_state * c_prev + _dot_tn(ka.astype(BF16), v)
        n_sc[hh_:hh_ + 1, :] = a_state * n_prev + jnp.sum(ka, axis=0, keepdims=True)
        m_sc[hh_:hh_ + 1, :] = jnp.broadcast_to(m_new, (1, m_sc.shape[1]))

    def step(c, carry):
        r0 = pl.multiple_of(c * L, L)
        for hh_ in range(hps):
            head_step(c, r0, hh_)
        return carry

    lax.fori_loop(0, nchunks, step, 0)
    c_ref[0] = c_sc[...]
    n_ref[0] = n_sc[...]
    m_ref[0] = m_sc[...]


def _mlstm(z2, gates, bias, g_mh, c0, n0, m0, batch, t_rows, chunk, valid, hps, zo):
    H, DK, DV = MLSTM_HEADS, MLSTM_QK_DIM, MLSTM_V_DIM
    nchunks = t_rows // chunk
    ng = H // hps
    g6 = gates.reshape(batch, t_rows, 2, ng, hps)
    gcol = g6.transpose(0, 3, 1, 4, 2).reshape(batch * ng * t_rows, 2 * hps)
    grow = g6.reshape(batch, nchunks, chunk, 2, ng, hps).transpose(0, 4, 5, 1, 3, 2)
    grow = grow.reshape(batch * ng, hps, nchunks, 2 * chunk)
    c0r = c0.reshape(batch * ng, hps, DK, DV)
    n0r = n0.reshape(batch * ng, hps, DK)
    m0r = jnp.broadcast_to(m0.reshape(batch * ng, hps, 1).astype(F32), (batch * ng, hps, LANE))
    kern = functools.partial(_mlstm_kernel, chunk=chunk, nchunks=nchunks, valid=valid, hps=hps)
    qk_w, v_w = hps * DK, hps * DV
    grp3 = lambda b, g, bias: (b * ng + g, 0, 0)
    grp4 = lambda b, g, bias: (b * ng + g, 0, 0, 0)
    grid_spec = pltpu.PrefetchScalarGridSpec(
        num_scalar_prefetch=1,
        grid=(batch, ng),
        in_specs=[
            pl.BlockSpec((t_rows, qk_w), lambda b, g, bias: (b, zo["mq"] // qk_w + g)),
            pl.BlockSpec((t_rows, qk_w), lambda b, g, bias: (b, zo["mk"] // qk_w + g)),
            pl.BlockSpec((t_rows, v_w), lambda b, g, bias: (b, zo["mv"] // v_w + g)),
            pl.BlockSpec((t_rows, v_w), lambda b, g, bias: (b, zo["mo"] // v_w + g)),
            pl.BlockSpec((t_rows, 2 * hps), lambda b, g, bias: (b * ng + g, 0)),
            pl.BlockSpec((1, hps, nchunks, 2 * chunk), grp4),
            pl.BlockSpec((1, v_w), lambda b, g, bias: (0, g)),
            pl.BlockSpec((1, hps, DK, DV), grp4),
            pl.BlockSpec((1, hps, DK), grp3),
            pl.BlockSpec((1, hps, LANE), grp3),
        ],
        out_specs=[
            pl.BlockSpec((t_rows, v_w), lambda b, g, bias: (b, g)),
            pl.BlockSpec((1, hps, DK, DV), grp4),
            pl.BlockSpec((1, hps, DK), grp3),
            pl.BlockSpec((1, hps, LANE), grp3),
        ],
        scratch_shapes=[pltpu.VMEM((hps, DK, DV), F32), pltpu.VMEM((hps, DK), F32), pltpu.VMEM((hps, LANE), F32)],
    )
    hg, c, n, m = pl.pallas_call(
        kern,
        grid_spec=grid_spec,
        out_shape=[
            jax.ShapeDtypeStruct((batch * t_rows, H * DV), BF16),
            jax.ShapeDtypeStruct((batch * ng, hps, DK, DV), F32),
            jax.ShapeDtypeStruct((batch * ng, hps, DK), F32),
            jax.ShapeDtypeStruct((batch * ng, hps, LANE), F32),
        ],
        compiler_params=_cparams(("parallel", "parallel")),
        name="mlstm",
    )(bias, z2, z2, z2, z2, gcol, grow, g_mh.reshape(1, H * DV), c0r, n0r, m0r)
    return hg, c.reshape(batch, H, DK, DV), n.reshape(batch, H, DK), m[:, :, 0].reshape(batch, H)


def _take_top(work, order, nsel, want_rank):
    rank = jnp.full(work.shape, NOT_A_RANK, F32) if want_rank else None
    vals, picks = [], []
    for a in range(nsel):
        m = jnp.max(work, axis=0, keepdims=True)
        am = jnp.min(jnp.where(work == m, order, ORDER_MAX), axis=0, keepdims=True)
        hit = order == am
        vals.append(m)
        picks.append(am)
        if want_rank:
            rank = jnp.where(hit, float(a), rank)
        work = jnp.where(hit, -jnp.inf, work)
    return vals, picks, rank


def _candidate_blocks(tm):
    K = PEER_TOPK
    blocks, pos, valid = [], [], []
    for a in range(CAND_ROW_BLOCKS):
        nb = K // (a + 1)
        rows = SUBLANE * ((nb + SUBLANE - 1) // SUBLANE)
        it = lax.broadcasted_iota(I32, (rows, tm), 0)
        blocks.append(("row", a, rows))
        pos.append(a * K + it)
        valid.append(it < nb)
    for b in range(K):
        na = K // (b + 1)
        if na <= CAND_ROW_BLOCKS:
            break
        rows = SUBLANE * ((na + SUBLANE - 1) // SUBLANE)
        it = lax.broadcasted_iota(I32, (rows, tm), 0)
        blocks.append(("col", b, rows))
        pos.append(it * K + b)
        valid.append((it >= CAND_ROW_BLOCKS) & (it < na))
    return blocks, jnp.concatenate(pos, axis=0), jnp.concatenate(valid, axis=0)


def _peer_select_kernel(pq_ref, keys_ref, e1_ref, ks_ref, e2_ref, rb_ref):
    K = PEER_TOPK
    tm = pq_ref.shape[0]
    key_row = lax.broadcasted_iota(I32, (PEER_N_KEYS, tm), 0)
    blocks, cand_pos, cand_ok = _candidate_blocks(tm)
    a_row = lax.broadcasted_iota(I32, (K, tm), 0)
    for h in range(PEER_HEADS):
        sides = []
        for p in range(2):
            qs = pq_ref[:, (2 * h + p) * LANE:(2 * h + p + 1) * LANE].astype(BF16)
            st = _dot_nt(keys_ref[2 * h + p], qs)
            vals, _, rank = _take_top(st, key_row, K, True)
            sides.append((st, vals, rank))
        (s1, v1, ra), (s2, v2, rb) = sides
        v1all = jnp.concatenate(v1, axis=0)
        v2all = jnp.concatenate(v2, axis=0)
        parts = [(v1[i] + v2all[:rows]) if kind == "row" else (v1all[:rows] + v2[i]) for kind, i, rows in blocks]
        cand = jnp.where(cand_ok, jnp.concatenate(parts, axis=0), -jnp.inf)
        top, pos, _ = _take_top(cand, cand_pos, K, False)
        zsum = jnp.ones_like(top[0])
        for k in range(1, K):
            zsum = zsum + jnp.exp(top[k] - top[0])
        count = jnp.zeros((K, tm), F32)
        for k in range(K):
            count = count + jnp.where(lax.shift_right_logical(pos[k], 4) == a_row, 1.0, 0.0)
        last_b = jnp.full((PEER_N_KEYS, tm), -1.0, F32)
        for a in range(K):
            last_b = jnp.where(ra == float(a), count[a:a + 1] - 1.0, last_b)
        e1_ref[h] = jnp.where(ra < float(K), jnp.exp(s1 - v1[0]), 0.0) / zsum
        ks_ref[h] = last_b
        e2_ref[h] = jnp.where(rb < float(K), jnp.exp(s2 - v2[0]), 0.0)
        rb_ref[h] = rb


def _peer_select(pq, keys, tm):
    n = pq.shape[0]
    big = jax.ShapeDtypeStruct((PEER_HEADS, PEER_N_KEYS, n), F32)
    bspec = pl.BlockSpec((PEER_HEADS, PEER_N_KEYS, tm), lambda i: (0, 0, i))
    return pl.pallas_call(
        _peer_select_kernel,
        grid=(n // tm,),
        in_specs=[pl.BlockSpec((tm, pq.shape[1]), lambda i: (i, 0)), pl.BlockSpec(keys.shape, lambda i: (0, 0, 0))],
        out_specs=[bspec] * 4,
        out_shape=[big] * 4,
        compiler_params=_cparams(("parallel",)),
        name="peer_select",
    )(pq, keys)


def _peer_expert_kernel(xn_ref, wd_ref, wu_ref, e1_ref, ks_ref, e2_ref, rb_ref, o_ref, coef_sc, *, krows):
    i = pl.program_id(1)
    n_tiles = pl.num_programs(1) - 1

    @pl.when(i == 0)
    def _():
        o_ref[...] = jnp.zeros(o_ref.shape, F32)
        coef_sc[1] = jnp.zeros(coef_sc.shape[1:], BF16)

    row0 = lax.rem(jnp.minimum(i, n_tiles - 1), SUBLANE // krows) * krows
    gates = []
    for r in range(krows):
        acc = None
        for h in range(PEER_HEADS):
            first = pl.ds(row0 + r, 1)
            term = jnp.where(rb_ref[h] <= ks_ref[h, first, :], e2_ref[h], 0.0) * e1_ref[h, first, :]
            acc = term if acc is None else acc + term
        gates.append(acc)
    gate = jnp.concatenate(gates, axis=0)

    o_ref[...] += _dot_tn(coef_sc[lax.rem(i + 1, 2)], wu_ref[...])
    fold = gate[0:SUBLANE]
    for q in range(1, gate.shape[0] // SUBLANE):
        fold = fold + gate[q * SUBLANE:(q + 1) * SUBLANE]
    fold1 = fold[:, 0:LANE]
    for q in range(1, fold.shape[1] // LANE):
        fold1 = fold1 + fold[:, q * LANE:(q + 1) * LANE]
    bits = pltpu.bitcast(fold1, jnp.uint32)
    zero = lax.shift_right_logical(lax.shift_right_logical(bits, jnp.uint32(16)), jnp.uint32(16))
    o_ref[0:SUBLANE, 0:LANE] += pltpu.bitcast(zero, F32)

    a_t = _dot_nt(wd_ref[...], xn_ref[...])
    gel = 0.5 * a_t * (1.0 + lax.erf(a_t * INV_SQRT2))
    coef_sc[lax.rem(i, 2)] = (gate * gel).astype(BF16)


def _peer_experts(xn, wd, wu, sel, tm, te):
    n, d = xn.shape
    ne = wd.shape[0]
    krows = te // PEER_N_KEYS
    n_tiles = ne // te
    per_blk = SUBLANE // krows
    e1, ks, e2, rb = sel
    cur = lambda i: jnp.minimum(i, n_tiles - 1)
    once = pl.Buffered(1)
    spec1 = pl.BlockSpec((PEER_HEADS, SUBLANE, tm), lambda j, i: (0, cur(i) // per_blk, j))
    spec2 = pl.BlockSpec((PEER_HEADS, PEER_N_KEYS, tm), lambda j, i: (0, 0, j), pipeline_mode=once)
    return pl.pallas_call(
        functools.partial(_peer_expert_kernel, krows=krows),
        grid=(pl.cdiv(n, tm), n_tiles + 1),
        in_specs=[
            pl.BlockSpec((tm, d), lambda j, i: (j, 0), pipeline_mode=once),
            pl.BlockSpec((te, d), lambda j, i: (cur(i), 0)),
            pl.BlockSpec((te, d), lambda j, i: (jnp.maximum(i - 1, 0), 0)),
            spec1, spec1, spec2, spec2,
        ],
        out_specs=pl.BlockSpec((tm, d), lambda j, i: (j, 0), pipeline_mode=once),
        out_shape=jax.ShapeDtypeStruct((n, d), F32),
        scratch_shapes=[pltpu.VMEM((2, te, tm), BF16)],
        compiler_params=_cparams(("parallel", "arbitrary")),
        name="peer_experts",
    )(xn, wd, wu, e1, ks, e2, rb)


def _rope_table(pos):
    inv = ROPE_THETA ** (-jnp.arange(0, ROPE_DIM, 2, dtype=F32) / ROPE_DIM)
    ang = pos.astype(F32)[:, None] * inv[None, :]
    reps = LANE // (ROPE_DIM // 2)
    return jnp.concatenate([jnp.tile(jnp.cos(ang), (1, reps)), jnp.tile(jnp.sin(ang), (1, reps))], axis=1)


def _rot_cols(w):
    half = ROPE_DIM // 2
    return jnp.concatenate([-w[..., half:], w[..., :half]], axis=-1)


def _pad_cols(w, width):
    return jnp.pad(w, [(0, 0)] * (w.ndim - 1) + [(0, width - w.shape[-1])])


def kernel(x_prompt, x_sample, p_prompt, p_sample, cache_kv_latent, cache_k_rope, state_mlstm_C, state_mlstm_n, state_mlstm_m, page_table, g_mix_norm, w_in, g_q_latent, w_uq, g_kv_latent, w_uk, w_uv, b_igate, b_fgate, g_mlstm_head, w_out, g_ffn_norm, peer_w_query, peer_sub_keys, peer_w_down, peer_w_up, g_ple_norm, w_ple, w_ple_gate, g_final_norm):
    B, S, D = x_prompt.shape
    DB, T, _ = x_sample.shape
    depth = w_in.shape[0]
    q_lora, kv_lora = g_q_latent.shape[1], g_kv_latent.shape[1]
    H, DK, DV = MLSTM_HEADS, MLSTM_QK_DIM, MLSTM_V_DIM
    n_p, n_s = B * S, DB * T
    n = n_p + n_s
    n_past = page_table.shape[1] * PAGE_SIZE
    scale = (NOPE_DIM + ROPE_DIM) ** -0.5
    t_pad = SUBLANE * ((T + SUBLANE - 1) // SUBLANE)

    zo, off = {}, 0
    for name, width in (("mq", H * DK), ("mk", H * DK), ("mv", H * DV), ("mo", H * DV), ("cq", q_lora),
                        ("ckv", kv_lora), ("kr", 2 * LANE), ("gate", LANE)):
        zo[name] = off
        off += width
    zw = 1024 * ((off + 1023) // 1024)

    cs = jnp.concatenate([jnp.tile(_rope_table(jnp.arange(S, dtype=I32)), (B, 1)),
                          jnp.tile(_rope_table(n_past + jnp.arange(T, dtype=I32)), (DB, 1))], axis=0)
    tmq = _tile(math.gcd(S, n_s), 128, BF16_SUBLANE)
    chunk_p = math.gcd(S, MLSTM_KERNEL_CHUNK)

    cache_kr_t = jnp.swapaxes(cache_k_rope, 2, 3)
    x = jnp.concatenate([x_prompt.reshape(n_p, D), x_sample.reshape(n_s, D)], axis=0)
    outs = {k: [] for k in ("ckv_p", "kr_p", "ckv_s", "kr_s", "C_p", "n_p", "m_p", "C_s", "n_s", "m_s")}
    for l in range(depth):
        sizes = (q_lora, kv_lora, ROPE_DIM, H * DK, H * DK, H * DV, H * DV, H, H)
        parts, st = [], 0
        for sz in sizes:
            parts.append(w_in[l][:, st:st + sz])
            st += sz
        wcq, wckv, wkr, wmq, wmk, wmv, wmo, wmi, wmf = parts
        w_z = jnp.concatenate([wmq, wmk, wmv, wmo, wcq, wckv, _pad_cols(wkr, LANE), _pad_cols(_rot_cols(wkr), LANE),
                               wmi, wmf], axis=1)
        w_z = _pad_cols(w_z, zw).astype(BF16)
        wq3 = w_uq[l].reshape(q_lora, MLA_HEADS, NOPE_DIM + ROPE_DIM)
        wn = wq3[:, :, :NOPE_DIM].reshape(q_lora, MLA_HEADS * NOPE_DIM).astype(BF16)
        wr = _pad_cols(wq3[:, :, NOPE_DIM:], LANE).reshape(q_lora, MLA_HEADS * LANE).astype(BF16)
        wrr = _pad_cols(_rot_cols(wq3[:, :, NOPE_DIM:]), LANE).reshape(q_lora, MLA_HEADS * LANE).astype(BF16)
        wuk = w_uk[l].reshape(kv_lora, MLA_HEADS, NOPE_DIM).transpose(1, 2, 0).astype(BF16)
        wuv = w_uv[l].reshape(kv_lora, MLA_HEADS, MLA_V_DIM).transpose(1, 0, 2).astype(BF16)
        bias = jnp.stack([b_igate[l], b_fgate[l]]).astype(F32)

        hn = _rmsnorm(x, g_mix_norm[l], BF16)
        z = _matmul(hn, w_z)

        q4, ckv, kr, kvb = _qprep(z, cs, g_q_latent[l], g_kv_latent[l], wn, wr, wrr, wuk, tmq, zo)
        o_mla_p = _attn_prompt(q4, kvb, wuv, B, S, tmq, scale)
        nblk_s = n_s // tmq
        qs = q4[n_p // tmq:].transpose(1, 0, 2, 3).reshape(MLA_HEADS, DB, T, KV_W).transpose(1, 0, 2, 3)
        qs = jnp.pad(qs, ((0, 0), (0, 0), (0, t_pad - T), (0, 0))).reshape(DB, MLA_HEADS * t_pad, KV_W)
        kvn = jnp.pad(kvb[n_p:].reshape(DB, T, KV_W), ((0, 0), (0, t_pad - T), (0, 0)))
        o_mla_s = _attn_sample(page_table, qs, kvn, wuv, cache_kv_latent, cache_kr_t, l, scale, T)
        o_mla = jnp.concatenate([o_mla_p, o_mla_s[:, :T].reshape(n_s, -1).astype(BF16)], axis=0)

        gz = z[:, zo["gate"]:zo["gate"] + 2 * H]
        c0p = jnp.zeros((B, H, DK, DV), F32)
        n0p = jnp.zeros((B, H, DK), F32)
        m0p = jnp.full((B, H), M_EMPTY, F32)
        hg_p, C_p, nn_p, mm_p = _mlstm(z, gz[:n_p].reshape(B, S, 2, H), bias, g_mlstm_head[l], c0p, n0p, m0p,
                                       B, S, chunk_p, chunk_p, MLSTM_HEADS_PER_STEP_PROMPT, zo)
        z_s = jnp.pad(z[n_p:].reshape(DB, T, zw), ((0, 0), (0, t_pad - T), (0, 0))).reshape(DB * t_pad, zw)
        g_s = jnp.pad(gz[n_p:].reshape(DB, T, 2, H), ((0, 0), (0, t_pad - T), (0, 0), (0, 0)))
        hg_s, C_s, nn_s, mm_s = _mlstm(z_s, g_s, bias, g_mlstm_head[l], state_mlstm_C[l].astype(F32),
                                       state_mlstm_n[l].astype(F32), state_mlstm_m[l], DB, t_pad, t_pad, T,
                                       MLSTM_HEADS_PER_STEP_SAMPLE, zo)
        o_ml = jnp.concatenate([hg_p, hg_s.reshape(DB, t_pad, H * DV)[:, :T].reshape(n_s, H * DV)], axis=0)

        x1 = _outproj(o_mla, o_ml, w_out[l].astype(BF16), x)

        xn = _rmsnorm(x1, g_ffn_norm[l], BF16)
        pq = _matmul(xn, peer_w_query[l].astype(BF16))
        keys = peer_sub_keys[l].reshape(2 * PEER_HEADS, PEER_N_KEYS, -1).astype(BF16)
        tms = _tile(n, 256, LANE)
        sel = _peer_select(pq, keys, tms)
        tmx = min(PEER_TOKEN_TILE, LANE * pl.cdiv(n, LANE))
        peer_out = _peer_experts(xn, peer_w_down[l].astype(BF16), peer_w_up[l].astype(BF16), sel, tmx, PEER_EXPERT_TILE)

        x2, xg = _add_rmsnorm(x1, peer_out, g_ple_norm[l])
        pcat = jnp.concatenate([p_prompt[l].reshape(n_p, -1), p_sample[l].reshape(n_s, -1)], axis=0).astype(BF16)
        x = _ple(xg, w_ple_gate[l].astype(BF16), pcat, w_ple[l].astype(BF16), x2)

        outs["ckv_p"].append(ckv[:n_p].reshape(B, S, kv_lora))
        outs["kr_p"].append(kr[:n_p].reshape(B, S, ROPE_DIM))
        outs["ckv_s"].append(ckv[n_p:].reshape(DB, T, kv_lora))
        outs["kr_s"].append(kr[n_p:].reshape(DB, T, ROPE_DIM))
        outs["C_p"].append(C_p)
        outs["n_p"].append(nn_p)
        outs["m_p"].append(mm_p)
        outs["C_s"].append(C_s.astype(state_mlstm_C.dtype))
        outs["n_s"].append(nn_s.astype(state_mlstm_n.dtype))
        outs["m_s"].append(mm_s.astype(state_mlstm_m.dtype))

    y = _rmsnorm(x, g_final_norm, F32)
    st = {k: jnp.stack(v) for k, v in outs.items()}
    return (y[:n_p].reshape(B, S, D), y[n_p:].reshape(DB, T, D), st["ckv_p"], st["kr_p"], st["ckv_s"], st["kr_s"],
            st["C_p"], st["n_p"], st["m_p"], st["C_s"], st["n_s"], st["m_s"])
```

```python
import functools
import math

import jax
import jax.numpy as jnp
from jax import lax
from jax.experimental import pallas as pl
from jax.experimental.pallas import tpu as pltpu

F32, BF16, I32 = jnp.float32, jnp.bfloat16, jnp.int32

MLA_HEADS = 16
NOPE_DIM = 128
ROPE_DIM = 64
MLA_V_DIM = 128
ROPE_THETA = 10000.0
MLSTM_HEADS = 8
MLSTM_QK_DIM = 128
MLSTM_V_DIM = 256
MLSTM_CHUNK = 64
M_EMPTY = -1e30
PEER_HEADS = 8
PEER_N_KEYS = 128
PEER_TOPK = 16
PAGE_SIZE = 128
EPS = 1e-6
INV_SQRT2 = 0.7071067811865476

LANE = 128
SUBLANE = 8
BF16_SUBLANE = 16
VMEM_LIMIT_BYTES = 60 * 1024 * 1024

KV_W = 640
NOT_A_RANK = 1.0e9
ORDER_MAX = float(2 ** 30)
CAND_ROW_BLOCKS = 4
MLSTM_KERNEL_CHUNK = 128
PEER_TOKEN_TILE = 512
PEER_EXPERT_TILE = 1024
MLSTM_HEADS_PER_STEP_PROMPT = 2
MLSTM_HEADS_PER_STEP_SAMPLE = 8


def _tile(n, cap, mult):
    best = None
    d = mult
    while d <= min(n, cap):
        if n % d == 0:
            best = d
        d += mult
    return n if best is None else best


def _cparams(sem):
    return pltpu.CompilerParams(dimension_semantics=sem, vmem_limit_bytes=VMEM_LIMIT_BYTES)


def _dot(a, b):
    return jnp.dot(a, b, preferred_element_type=F32)


def _dot_nt(a, b):
    return lax.dot_general(a, b, (((1,), (1,)), ((), ())), preferred_element_type=F32)


def _dot_tn(a, b):
    return lax.dot_general(a, b, (((0,), (0,)), ((), ())), preferred_element_type=F32)


def _rms(x, g):
    return x * lax.rsqrt(jnp.mean(x * x, axis=-1, keepdims=True) + EPS) * g


def _norm_kernel(x_ref, g_ref, o_ref):
    o_ref[...] = _rms(x_ref[...], g_ref[...]).astype(o_ref.dtype)


def _rmsnorm(x, g, out_dtype, row0=0, nrows=None):
    d = x.shape[1]
    n = x.shape[0] if nrows is None else nrows
    tm = _tile(math.gcd(row0, n), 256, BF16_SUBLANE)
    first = row0 // tm
    return pl.pallas_call(
        _norm_kernel,
        grid=(n // tm,),
        in_specs=[pl.BlockSpec((tm, d), lambda i: (first + i, 0)), pl.BlockSpec((1, d), lambda i: (0, 0))],
        out_specs=pl.BlockSpec((tm, d), lambda i: (i, 0)),
        out_shape=jax.ShapeDtypeStruct((n, d), out_dtype),
        compiler_params=_cparams(("parallel",)),
        name="rmsnorm",
    )(x, g.reshape(1, d))


def _add_norm_kernel(a_ref, b_ref, g_ref, s_ref, o_ref):
    s = a_ref[...] + b_ref[...]
    s_ref[...] = s
    o_ref[...] = _rms(s, g_ref[...]).astype(o_ref.dtype)


def _add_rmsnorm(a, b, g):
    n, d = a.shape
    tm = _tile(n, 128, BF16_SUBLANE)
    row = pl.BlockSpec((tm, d), lambda i: (i, 0))
    return pl.pallas_call(
        _add_norm_kernel,
        grid=(n // tm,),
        in_specs=[row, row, pl.BlockSpec((1, d), lambda i: (0, 0))],
        out_specs=[row, row],
        out_shape=[jax.ShapeDtypeStruct((n, d), F32), jax.ShapeDtypeStruct((n, d), BF16)],
        compiler_params=_cparams(("parallel",)),
        name="add_rmsnorm",
    )(a, b, g.reshape(1, d))


def _mm_kernel(a_ref, b_ref, o_ref):
    o_ref[...] = _dot(a_ref[...], b_ref[...]).astype(o_ref.dtype)


def _matmul(a, b, out_dtype=F32):
    m, k = a.shape
    n = b.shape[1]
    tm = _tile(m, 1088, BF16_SUBLANE)
    tn = _tile(n, 1024, LANE)
    return pl.pallas_call(
        _mm_kernel,
        grid=(m // tm, n // tn),
        in_specs=[pl.BlockSpec((tm, k), lambda i, j: (i, 0)), pl.BlockSpec((k, tn), lambda i, j: (0, j))],
        out_specs=pl.BlockSpec((tm, tn), lambda i, j: (i, j)),
        out_shape=jax.ShapeDtypeStruct((m, n), out_dtype),
        compiler_params=_cparams(("parallel", "parallel")),
        name="matmul",
    )(a, b)


def _outproj_kernel(a1_ref, a2_ref, b1_ref, b2_ref, r_ref, o_ref):
    o_ref[...] = r_ref[...] + (_dot(a1_ref[...], b1_ref[...]) + _dot(a2_ref[...], b2_ref[...]))


def _outproj(a1, a2, w, r):
    m, k1 = a1.shape
    k2 = a2.shape[1]
    n = w.shape[1]
    tm = _tile(m, 1088, BF16_SUBLANE)
    tn = _tile(n, 512, LANE)
    nb1 = k1 // k2
    return pl.pallas_call(
        _outproj_kernel,
        grid=(m // tm, n // tn),
        in_specs=[
            pl.BlockSpec((tm, k1), lambda i, j: (i, 0)),
            pl.BlockSpec((tm, k2), lambda i, j: (i, 0)),
            pl.BlockSpec((k1, tn), lambda i, j: (0, j)),
            pl.BlockSpec((k2, tn), lambda i, j: (nb1, j)),
            pl.BlockSpec((tm, tn), lambda i, j: (i, j)),
        ],
        out_specs=pl.BlockSpec((tm, tn), lambda i, j: (i, j)),
        out_shape=jax.ShapeDtypeStruct((m, n), F32),
        compiler_params=_cparams(("parallel", "parallel")),
        name="outproj",
    )(a1, a2, w, w, r)


def _ple_kernel(xg_ref, wg_ref, p_ref, wp_ref, x_ref, o_ref):
    gate = _dot(xg_ref[...], wg_ref[...])
    o_ref[...] = x_ref[...] + _dot(p_ref[...], wp_ref[...]) * jax.nn.sigmoid(gate)


def _ple(xg, w_gate, p, w_ple, x):
    m, k = xg.shape
    n = w_gate.shape[1]
    kp = p.shape[1]
    tm = _tile(m, 1088, BF16_SUBLANE)
    tn = _tile(n, 512, LANE)
    return pl.pallas_call(
        _ple_kernel,
        grid=(m // tm, n // tn),
        in_specs=[
            pl.BlockSpec((tm, k), lambda i, j: (i, 0)),
            pl.BlockSpec((k, tn), lambda i, j: (0, j)),
            pl.BlockSpec((tm, kp), lambda i, j: (i, 0)),
            pl.BlockSpec((kp, tn), lambda i, j: (0, j)),
            pl.BlockSpec((tm, tn), lambda i, j: (i, j)),
        ],
        out_specs=pl.BlockSpec((tm, tn), lambda i, j: (i, j)),
        out_shape=jax.ShapeDtypeStruct((m, n), F32),
        compiler_params=_cparams(("parallel", "parallel")),
        name="ple",
    )(xg, w_gate, p, w_ple, x)


def _qprep_kernel(cq_ref, ckv_ref, krk_ref, cs_ref, gq_ref, gkv_ref, wn_ref, wr_ref, wrr_ref, wuk_ref,
                  q_ref, ckv_out_ref, kr_out_ref, kv_ref):
    cqn = _rms(cq_ref[...], gq_ref[...]).astype(BF16)
    cs = cs_ref[...]
    cos1, sin1 = cs[:, :LANE], cs[:, LANE:]
    cos_t = jnp.concatenate([cos1] * MLA_HEADS, axis=1)
    sin_t = jnp.concatenate([sin1] * MLA_HEADS, axis=1)
    qn = _dot(cqn, wn_ref[...])
    qr = _dot(cqn, wr_ref[...]) * cos_t + _dot(cqn, wrr_ref[...]) * sin_t
    for h in range(MLA_HEADS):
        sl = slice(h * LANE, (h + 1) * LANE)
        qa = _dot(qn[:, sl].astype(BF16), wuk_ref[h])
        q_ref[0, h, :, 0:512] = qa.astype(BF16)
        q_ref[0, h, :, 512:KV_W] = qr[:, sl].astype(BF16)
    c = _rms(ckv_ref[...], gkv_ref[...])
    ckv_out_ref[...] = c
    krk = krk_ref[...]
    kr = krk[:, :LANE] * cos1 + krk[:, LANE:] * sin1
    kr_out_ref[...] = kr[:, :ROPE_DIM]
    kv_ref[:, 0:512] = c.astype(BF16)
    kv_ref[:, 512:KV_W] = kr.astype(BF16)


def _qprep(z, cs, g_q, g_kv, wn, wr, wrr, wuk, tm, zo):
    n = z.shape[0]
    q_lora = wn.shape[0]
    kv_lora = wuk.shape[2]
    const2 = lambda i: (0, 0)
    return pl.pallas_call(
        _qprep_kernel,
        grid=(n // tm,),
        in_specs=[
            pl.BlockSpec((tm, q_lora), lambda i: (i, zo["cq"] // q_lora)),
            pl.BlockSpec((tm, kv_lora), lambda i: (i, zo["ckv"] // kv_lora)),
            pl.BlockSpec((tm, 2 * LANE), lambda i: (i, zo["kr"] // (2 * LANE))),
            pl.BlockSpec((tm, 2 * LANE), lambda i: (i, 0)),
            pl.BlockSpec((1, q_lora), const2),
            pl.BlockSpec((1, kv_lora), const2),
            pl.BlockSpec(wn.shape, const2),
            pl.BlockSpec(wr.shape, const2),
            pl.BlockSpec(wrr.shape, const2),
            pl.BlockSpec(wuk.shape, lambda i: (0, 0, 0)),
        ],
        out_specs=[
            pl.BlockSpec((1, MLA_HEADS, tm, KV_W), lambda i: (i, 0, 0, 0)),
            pl.BlockSpec((tm, kv_lora), lambda i: (i, 0)),
            pl.BlockSpec((tm, ROPE_DIM), lambda i: (i, 0)),
            pl.BlockSpec((tm, KV_W), lambda i: (i, 0)),
        ],
        out_shape=[
            jax.ShapeDtypeStruct((n // tm, MLA_HEADS, tm, KV_W), BF16),
            jax.ShapeDtypeStruct((n, kv_lora), F32),
            jax.ShapeDtypeStruct((n, ROPE_DIM), F32),
            jax.ShapeDtypeStruct((n, KV_W), BF16),
        ],
        compiler_params=_cparams(("parallel",)),
        name="mla_qprep",
    )(z, z, z, cs, g_q.reshape(1, -1), g_kv.reshape(1, -1), wn, wr, wrr, wuk)


def _softmax_step(s, v, m_sc, l_sc, acc_sc):
    m_old = m_sc[...]
    m_new = jnp.maximum(m_old, jnp.max(s, axis=-1, keepdims=True))
    alpha = jnp.exp(m_old - m_new)
    p = jnp.exp(s - m_new)
    l_sc[...] = alpha * l_sc[...] + jnp.sum(p, axis=-1, keepdims=True)
    acc_sc[...] = alpha * acc_sc[...] + _dot(p.astype(BF16), v)
    m_sc[...] = m_new


def _attn_prompt_kernel(q_ref, kv_ref, wuv_ref, o_ref, m_sc, l_sc, acc_sc, s_sc, *, tq, tk, scale, kv_lora):
    qi = pl.program_id(1)
    rows = MLA_HEADS * tq
    q = q_ref[0].reshape(rows, KV_W)
    m_sc[...] = jnp.full(m_sc.shape, -jnp.inf, F32)
    l_sc[...] = jnp.zeros(l_sc.shape, F32)
    acc_sc[...] = jnp.zeros(acc_sc.shape, F32)
    nfull = (qi * tq) // tk

    def keys(j):
        return kv_ref[pl.ds(pl.multiple_of(j * tk, tk), tk), :]

    s_sc[0] = _dot_nt(q, keys(0)) * scale

    def body(j, carry):
        s = s_sc[lax.rem(j, 2)]
        s_sc[lax.rem(j + 1, 2)] = _dot_nt(q, keys(j + 1)) * scale
        _softmax_step(s, keys(j)[:, :kv_lora], m_sc, l_sc, acc_sc)
        return carry

    lax.fori_loop(0, nfull, body, 0)
    qpos = qi * tq + lax.rem(lax.broadcasted_iota(I32, (rows, 1), 0), tq)
    kpos = nfull * tk + lax.broadcasted_iota(I32, (1, tk), 1)
    s = jnp.where(kpos <= qpos, s_sc[lax.rem(nfull, 2)], -jnp.inf)
    _softmax_step(s, keys(nfull)[:, :kv_lora], m_sc, l_sc, acc_sc)
    o = acc_sc[...] / l_sc[...]
    for h in range(MLA_HEADS):
        oh = o[h * tq:(h + 1) * tq].astype(BF16)
        o_ref[:, h * MLA_V_DIM:(h + 1) * MLA_V_DIM] = _dot(oh, wuv_ref[h]).astype(o_ref.dtype)


def _attn_prompt(q4, kv, wuv, batch, seq, tq, scale):
    kv_lora = wuv.shape[1]
    nq = seq // tq
    tk = _tile(seq, 512, tq)
    rows = MLA_HEADS * tq
    kern = functools.partial(_attn_prompt_kernel, tq=tq, tk=tk, scale=scale, kv_lora=kv_lora)
    return pl.pallas_call(
        kern,
        grid=(batch, nq),
        in_specs=[
            pl.BlockSpec((1, MLA_HEADS, tq, KV_W), lambda b, i: (b * nq + i, 0, 0, 0)),
            pl.BlockSpec((seq, KV_W), lambda b, i: (b, 0)),
            pl.BlockSpec(wuv.shape, lambda b, i: (0, 0, 0)),
        ],
        out_specs=pl.BlockSpec((tq, MLA_HEADS * MLA_V_DIM), lambda b, i: (b * nq + i, 0)),
        out_shape=jax.ShapeDtypeStruct((batch * seq, MLA_HEADS * MLA_V_DIM), BF16),
        scratch_shapes=[pltpu.VMEM((rows, 1), F32), pltpu.VMEM((rows, 1), F32), pltpu.VMEM((rows, kv_lora), F32),
                        pltpu.VMEM((2, rows, tk), F32)],
        compiler_params=_cparams(("parallel", "arbitrary")),
        name="mla_prompt_attention",
    )(q4, kv, wuv)


def _page_copies(pt_ref, ckv_hbm, ckr_hbm, cin, rin, sem_c, sem_r, seq, chunk, slot, *, pp, layer):
    page = ckv_hbm.shape[2]
    copies = []
    for i in range(pp):
        pid = pt_ref[seq, chunk * pp + i]
        copies.append(pltpu.make_async_copy(ckv_hbm.at[layer, pid], cin.at[slot, pl.ds(i * page, page)], sem_c.at[slot]))
        copies.append(pltpu.make_async_copy(ckr_hbm.at[layer, pid], rin.at[slot, i], sem_r.at[slot]))
    return copies


def _attn_sample_kernel(pt_ref, q_ref, kvn_ref, wuv_ref, ckv_hbm, ckr_hbm, o_ref, cin, rin, sem_c, sem_r, cbuf, rbuf,
                        m_sc, l_sc, acc_sc, *, pp, layer, scale, t_valid, t_pad, kv_lora):
    s_id, c = pl.program_id(0), pl.program_id(1)
    nchunk = pl.num_programs(1)
    k = s_id * nchunk + c
    slot = lax.rem(k, 2)
    page = ckv_hbm.shape[2]
    copies = functools.partial(_page_copies, pt_ref, ckv_hbm, ckr_hbm, cin, rin, sem_c, sem_r, pp=pp, layer=layer)

    @pl.when(k == 0)
    def _():
        for j, cp in enumerate(copies(s_id, c, slot)):
            cp.start(priority=(j // 2) % 2)

    @pl.when(k + 1 < pl.num_programs(0) * nchunk)
    def _():
        wrap = c + 1 == nchunk
        nxt_s = jnp.where(wrap, s_id + 1, s_id)
        nxt_c = jnp.where(wrap, 0, c + 1)
        for j, cp in enumerate(copies(nxt_s, nxt_c, 1 - slot)):
            cp.start(priority=(j // 2) % 2)

    @pl.when(c == 0)
    def _():
        m_sc[...] = jnp.full(m_sc.shape, -jnp.inf, F32)
        l_sc[...] = jnp.zeros(l_sc.shape, F32)
        acc_sc[...] = jnp.zeros(acc_sc.shape, F32)

    for cp in copies(s_id, c, slot):
        cp.wait()
    for i in range(pp):
        cbuf[i * page:(i + 1) * page, :] = cin[slot, pl.ds(i * page, page), :].astype(BF16)
        rbuf[:, i * page:(i + 1) * page] = rin[slot, i].astype(BF16)
    q = q_ref[0]
    kc = cbuf[...]
    s = _dot_nt(q[:, :kv_lora], kc) + _dot(q[:, kv_lora:kv_lora + ROPE_DIM], rbuf[...])
    _softmax_step(s * scale, kc, m_sc, l_sc, acc_sc)

    @pl.when(c == pl.num_programs(1) - 1)
    def _():
        kn = kvn_ref[0]
        s = _dot_nt(q, kn) * scale
        qt = lax.rem(lax.broadcasted_iota(I32, (q.shape[0], 1), 0), t_pad)
        kt = lax.broadcasted_iota(I32, (1, t_pad), 1)
        s = jnp.where((kt <= qt) & (kt < t_valid), s, -jnp.inf)
        _softmax_step(s, kn[:, :kv_lora], m_sc, l_sc, acc_sc)
        o = acc_sc[...] / l_sc[...]
        for h in range(MLA_HEADS):
            oh = o[h * t_pad:(h + 1) * t_pad].astype(BF16)
            o_ref[0, :, h * MLA_V_DIM:(h + 1) * MLA_V_DIM] = _dot(oh, wuv_ref[h])


def _attn_sample(page_table, qs, kvn, wuv, cache_kv, cache_kr_t, layer, scale, t_valid):
    nseq, rows, _ = qs.shape
    t_pad = kvn.shape[1]
    npages = page_table.shape[1]
    page, kv_lora = cache_kv.shape[2], cache_kv.shape[3]
    pp = _tile(npages, 32, 1)
    kern = functools.partial(_attn_sample_kernel, pp=pp, layer=layer, scale=scale, t_valid=t_valid, t_pad=t_pad,
                             kv_lora=kv_lora)
    grid_spec = pltpu.PrefetchScalarGridSpec(
        num_scalar_prefetch=1,
        grid=(nseq, npages // pp),
        in_specs=[
            pl.BlockSpec((1, rows, KV_W), lambda s, c, pt: (s, 0, 0)),
            pl.BlockSpec((1, t_pad, KV_W), lambda s, c, pt: (s, 0, 0)),
            pl.BlockSpec(wuv.shape, lambda s, c, pt: (0, 0, 0)),
            pl.BlockSpec(memory_space=pl.ANY),
            pl.BlockSpec(memory_space=pl.ANY),
        ],
        out_specs=pl.BlockSpec((1, t_pad, MLA_HEADS * MLA_V_DIM), lambda s, c, pt: (s, 0, 0)),
        scratch_shapes=[
            pltpu.VMEM((2, pp * page, kv_lora), F32),
            pltpu.VMEM((2, pp, ROPE_DIM, page), F32),
            pltpu.SemaphoreType.DMA((2,)),
            pltpu.SemaphoreType.DMA((2,)),
            pltpu.VMEM((pp * page, kv_lora), BF16),
            pltpu.VMEM((ROPE_DIM, pp * page), BF16),
            pltpu.VMEM((rows, 1), F32),
            pltpu.VMEM((rows, 1), F32),
            pltpu.VMEM((rows, kv_lora), F32),
        ],
    )
    return pl.pallas_call(
        kern,
        grid_spec=grid_spec,
        out_shape=jax.ShapeDtypeStruct((nseq, t_pad, MLA_HEADS * MLA_V_DIM), F32),
        compiler_params=_cparams(("arbitrary", "arbitrary")),
        name="mla_sample_attention",
    )(page_table, qs, kvn, wuv, cache_kv, cache_kr_t)


def _mlstm_kernel(bias_ref, q_ref, k_ref, v_ref, mo_ref, gcol_ref, grow_ref, gmh_ref, c0_ref, n0_ref, m0_ref,
                  h_ref, c_ref, n_ref, m_ref, c_sc, n_sc, m_sc, *, chunk, nchunks, valid, hps):
    DK, DV = MLSTM_QK_DIM, MLSTM_V_DIM
    hg = pl.program_id(1)
    c_sc[...] = c0_ref[0]
    n_sc[...] = n0_ref[0]
    m_sc[...] = m0_ref[0]
    L = chunk
    li = lax.broadcasted_iota(I32, (L, L), 0)
    si = lax.broadcasted_iota(I32, (L, L), 1)
    causal = si <= li
    col_ok = lax.broadcasted_iota(I32, (L, 1), 0) < valid
    row_ok = lax.broadcasted_iota(I32, (1, L), 1) < valid
    kscale = MLSTM_QK_DIM ** -0.5

    def head_step(c, r0, hh_):
        qk = slice(hh_ * DK, (hh_ + 1) * DK)
        vv = slice(hh_ * DV, (hh_ + 1) * DV)
        b_i, b_f = bias_ref[0, hg * hps + hh_], bias_ref[1, hg * hps + hh_]
        q = q_ref[pl.ds(r0, L), qk]
        ks = k_ref[pl.ds(r0, L), qk] * kscale
        v = v_ref[pl.ds(r0, L), vv].astype(BF16)
        gc = gcol_ref[pl.ds(r0, L), 2 * hh_:2 * hh_ + 2]
        gr = grow_ref[0, hh_, pl.ds(c, 1), :]
        ig_col = jnp.where(col_ok, gc[:, 0:1] + b_i, -jnp.inf)
        ig_row = jnp.where(row_ok, gr[:, 0:L] + b_i, -jnp.inf)
        fl_col = jnp.where(col_ok, jax.nn.log_sigmoid(gc[:, 1:2] + b_f), 0.0)
        fl_row = jnp.where(row_ok, jax.nn.log_sigmoid(gr[:, L:2 * L] + b_f), 0.0)
        b_col = jnp.sum(jnp.where(causal, fl_row, 0.0), axis=1, keepdims=True)
        b_row = jnp.sum(jnp.where(li <= si, fl_col, 0.0), axis=0, keepdims=True)
        m_prev = m_sc[hh_:hh_ + 1, 0:1]
        n_prev = n_sc[hh_:hh_ + 1, :]
        c_prev = c_sc[hh_]
        dmat = jnp.where(causal, b_col - b_row + ig_row, -jnp.inf)
        m_inter = b_col + m_prev
        m_t = jnp.maximum(m_inter, jnp.max(dmat, axis=1, keepdims=True))
        w_inter = jnp.exp(m_inter - m_t)
        qb = q.astype(BF16)
        s = _dot_nt(qb, ks.astype(BF16)) * jnp.exp(dmat - m_t)
        num = w_inter * _dot(qb, c_prev.astype(BF16)) + _dot(s.astype(BF16), v)
        den = w_inter * jnp.sum(q * n_prev, axis=1, keepdims=True) + jnp.sum(s, axis=1, keepdims=True)
        hh = num / jnp.maximum(jnp.abs(den), jnp.exp(-m_t))
        y = _rms(hh, gmh_ref[:, vv])
        h_ref[pl.ds(r0, L), vv] = (y * jax.nn.sigmoid(mo_ref[pl.ds(r0, L), vv])).astype(h_ref.dtype)
        b_last = jnp.sum(fl_row, axis=1, keepdims=True)
        w_end_col = b_last - b_col + ig_col
        w_end_row = b_last - b_row + ig_row
        m_new = jnp.maximum(b_last + m_prev, jnp.max(w_end_row, axis=1, keepdims=True))
        a_state = jnp.exp(b_last + m_prev - m_new)
        ka = ks * jnp.exp(w_end_col - m_new)
        c_sc[hh_] = a_state * c_prev + _dot_tn(ka.astype(BF16), v)
        n_sc[hh_:hh_ + 1, :] = a_state * n_prev + jnp.sum(ka, axis=0, keepdims=True)
        m_sc[hh_:hh_ + 1, :] = jnp.broadcast_to(m_new, (1, m_sc.shape[1]))

    def step(c, carry):
        r0 = pl.multiple_of(c * L, L)
        for hh_ in range(hps):
            head_step(c, r0, hh_)
        return carry

    lax.fori_loop(0, nchunks, step, 0)
    c_ref[0] = c_sc[...]
    n_ref[0] = n_sc[...]
    m_ref[0] = m_sc[...]


def _mlstm(z2, gates, bias, g_mh, c0, n0, m0, batch, t_rows, chunk, valid, hps, zo):
    H, DK, DV = MLSTM_HEADS, MLSTM_QK_DIM, MLSTM_V_DIM
    nchunks = t_rows // chunk
    ng = H // hps
    g6 = gates.reshape(batch, t_rows, 2, ng, hps)
    gcol = g6.transpose(0, 3, 1, 4, 2).reshape(batch * ng * t_rows, 2 * hps)
    grow = g6.reshape(batch, nchunks, chunk, 2, ng, hps).transpose(0, 4, 5, 1, 3, 2)
    grow = grow.reshape(batch * ng, hps, nchunks, 2 * chunk)
    c0r = c0.reshape(batch * ng, hps, DK, DV)
    n0r = n0.reshape(batch * ng, hps, DK)
    m0r = jnp.broadcast_to(m0.reshape(batch * ng, hps, 1).astype(F32), (batch * ng, hps, LANE))
    kern = functools.partial(_mlstm_kernel, chunk=chunk, nchunks=nchunks, valid=valid, hps=hps)
    qk_w, v_w = hps * DK, hps * DV
    grp3 = lambda b, g, bias: (b * ng + g, 0, 0)
    grp4 = lambda b, g, bias: (b * ng + g, 0, 0, 0)
    grid_spec = pltpu.PrefetchScalarGridSpec(
        num_scalar_prefetch=1,
        grid=(batch, ng),
        in_specs=[
            pl.BlockSpec((t_rows, qk_w), lambda b, g, bias: (b, zo["mq"] // qk_w + g)),
            pl.BlockSpec((t_rows, qk_w), lambda b, g, bias: (b, zo["mk"] // qk_w + g)),
            pl.BlockSpec((t_rows, v_w), lambda b, g, bias: (b, zo["mv"] // v_w + g)),
            pl.BlockSpec((t_rows, v_w), lambda b, g, bias: (b, zo["mo"] // v_w + g)),
            pl.BlockSpec((t_rows, 2 * hps), lambda b, g, bias: (b * ng + g, 0)),
            pl.BlockSpec((1, hps, nchunks, 2 * chunk), grp4),
            pl.BlockSpec((1, v_w), lambda b, g, bias: (0, g)),
            pl.BlockSpec((1, hps, DK, DV), grp4),
            pl.BlockSpec((1, hps, DK), grp3),
            pl.BlockSpec((1, hps, LANE), grp3),
        ],
        out_specs=[
            pl.BlockSpec((t_rows, v_w), lambda b, g, bias: (b, g)),
            pl.BlockSpec((1, hps, DK, DV), grp4),
            pl.BlockSpec((1, hps, DK), grp3),
            pl.BlockSpec((1, hps, LANE), grp3),
        ],
        scratch_shapes=[pltpu.VMEM((hps, DK, DV), F32), pltpu.VMEM((hps, DK), F32), pltpu.VMEM((hps, LANE), F32)],
    )
    hg, c, n, m = pl.pallas_call(
        kern,
        grid_spec=grid_spec,
        out_shape=[
            jax.ShapeDtypeStruct((batch * t_rows, H * DV), BF16),
            jax.ShapeDtypeStruct((batch * ng, hps, DK, DV), F32),
            jax.ShapeDtypeStruct((batch * ng, hps, DK), F32),
            jax.ShapeDtypeStruct((batch * ng, hps, LANE), F32),
        ],
        compiler_params=_cparams(("parallel", "parallel")),
        name="mlstm",
    )(bias, z2, z2, z2, z2, gcol, grow, g_mh.reshape(1, H * DV), c0r, n0r, m0r)
    return hg, c.reshape(batch, H, DK, DV), n.reshape(batch, H, DK), m[:, :, 0].reshape(batch, H)


def _take_top(work, order, nsel, want_rank):
    rank = jnp.full(work.shape, NOT_A_RANK, F32) if want_rank else None
    vals, picks = [], []
    for a in range(nsel):
        m = jnp.max(work, axis=0, keepdims=True)
        am = jnp.min(jnp.where(work == m, order, ORDER_MAX), axis=0, keepdims=True)
        hit = order == am
        vals.append(m)
        picks.append(am)
        if want_rank:
            rank = jnp.where(hit, float(a), rank)
        work = jnp.where(hit, -jnp.inf, work)
    return vals, picks, rank


def _candidate_blocks(tm):
    K = PEER_TOPK
    blocks, pos, valid = [], [], []
    for a in range(CAND_ROW_BLOCKS):
        nb = K // (a + 1)
        rows = SUBLANE * ((nb + SUBLANE - 1) // SUBLANE)
        it = lax.broadcasted_iota(I32, (rows, tm), 0)
        blocks.append(("row", a, rows))
        pos.append(a * K + it)
        valid.append(it < nb)
    for b in range(K):
        na = K // (b + 1)
        if na <= CAND_ROW_BLOCKS:
            break
        rows = SUBLANE * ((na + SUBLANE - 1) // SUBLANE)
        it = lax.broadcasted_iota(I32, (rows, tm), 0)
        blocks.append(("col", b, rows))
        pos.append(it * K + b)
        valid.append((it >= CAND_ROW_BLOCKS) & (it < na))
    return blocks, jnp.concatenate(pos, axis=0).astype(F32), jnp.concatenate(valid, axis=0)


def _peer_select_kernel(pq_ref, keys_ref, e1_ref, ks_ref, e2_ref, rb_ref):
    K = PEER_TOPK
    tm = pq_ref.shape[0]
    key_row = lax.broadcasted_iota(I32, (PEER_N_KEYS, tm), 0).astype(F32)
    blocks, cand_pos, cand_ok = _candidate_blocks(tm)
    a_row = lax.broadcasted_iota(I32, (K, tm), 0)
    for h in range(PEER_HEADS):
        sides = []
        for p in range(2):
            qs = pq_ref[:, (2 * h + p) * LANE:(2 * h + p + 1) * LANE].astype(BF16)
            st = _dot_nt(keys_ref[2 * h + p], qs)
            vals, _, rank = _take_top(st, key_row, K, True)
            sides.append((st, vals, rank))
        (s1, v1, ra), (s2, v2, rb) = sides
        v1all = jnp.concatenate(v1, axis=0)
        v2all = jnp.concatenate(v2, axis=0)
        parts = [(v1[i] + v2all[:rows]) if kind == "row" else (v1all[:rows] + v2[i]) for kind, i, rows in blocks]
        cand = jnp.where(cand_ok, jnp.concatenate(parts, axis=0), -jnp.inf)
        top, pos, _ = _take_top(cand, cand_pos, K, False)
        zsum = jnp.ones_like(top[0])
        for k in range(1, K):
            zsum = zsum + jnp.exp(top[k] - top[0])
        count = jnp.zeros((K, tm), F32)
        for k in range(K):
            count = count + jnp.where(lax.shift_right_logical(pos[k].astype(I32), 4) == a_row, 1.0, 0.0)
        last_b = jnp.full((PEER_N_KEYS, tm), -1.0, F32)
        for a in range(K):
            last_b = jnp.where(ra == float(a), count[a:a + 1] - 1.0, last_b)
        e1_ref[h] = jnp.where(ra < float(K), jnp.exp(s1 - v1[0]), 0.0) / zsum
        ks_ref[h] = last_b
        e2_ref[h] = jnp.where(rb < float(K), jnp.exp(s2 - v2[0]), 0.0)
        rb_ref[h] = rb


def _peer_select(pq, keys, tm):
    n = pq.shape[0]
    big = jax.ShapeDtypeStruct((PEER_HEADS, PEER_N_KEYS, n), F32)
    bspec = pl.BlockSpec((PEER_HEADS, PEER_N_KEYS, tm), lambda i: (0, 0, i))
    return pl.pallas_call(
        _peer_select_kernel,
        grid=(n // tm,),
        in_specs=[pl.BlockSpec((tm, pq.shape[1]), lambda i: (i, 0)), pl.BlockSpec(keys.shape, lambda i: (0, 0, 0))],
        out_specs=[bspec] * 4,
        out_shape=[big] * 4,
        compiler_params=_cparams(("parallel",)),
        name="peer_select",
    )(pq, keys)


def _peer_expert_kernel(xn_ref, wd_ref, wu_ref, e1_ref, ks_ref, e2_ref, rb_ref, o_ref, coef_sc, *, krows):
    i = pl.program_id(1)
    n_tiles = pl.num_programs(1) - 1

    @pl.when(i == 0)
    def _():
        o_ref[...] = jnp.zeros(o_ref.shape, F32)
        coef_sc[1] = jnp.zeros(coef_sc.shape[1:], BF16)

    row0 = lax.rem(jnp.minimum(i, n_tiles - 1), SUBLANE // krows) * krows
    gates = []
    for r in range(krows):
        acc = None
        for h in range(PEER_HEADS):
            first = pl.ds(row0 + r, 1)
            term = jnp.where(rb_ref[h] <= ks_ref[h, first, :], e2_ref[h], 0.0) * e1_ref[h, first, :]
            acc = term if acc is None else acc + term
        gates.append(acc)
    gate = jnp.concatenate(gates, axis=0)

    o_ref[...] += _dot_tn(coef_sc[lax.rem(i + 1, 2)], wu_ref[...])
    fold = gate[0:SUBLANE]
    for q in range(1, gate.shape[0] // SUBLANE):
        fold = fold + gate[q * SUBLANE:(q + 1) * SUBLANE]
    fold1 = fold[:, 0:LANE]
    for q in range(1, fold.shape[1] // LANE):
        fold1 = fold1 + fold[:, q * LANE:(q + 1) * LANE]
    bits = pltpu.bitcast(fold1, jnp.uint32)
    zero = lax.shift_right_logical(lax.shift_right_logical(bits, jnp.uint32(16)), jnp.uint32(16))
    o_ref[0:SUBLANE, 0:LANE] += pltpu.bitcast(zero, F32)

    a_t = _dot_nt(wd_ref[...], xn_ref[...])
    gel = 0.5 * a_t * (1.0 + lax.erf(a_t * INV_SQRT2))
    coef_sc[lax.rem(i, 2)] = (gate * gel).astype(BF16)


def _peer_experts(xn, wd, wu, sel, tm, te):
    n, d = xn.shape
    ne = wd.shape[0]
    krows = te // PEER_N_KEYS
    n_tiles = ne // te
    per_blk = SUBLANE // krows
    e1, ks, e2, rb = sel
    cur = lambda i: jnp.minimum(i, n_tiles - 1)
    once = pl.Buffered(1)
    spec1 = pl.BlockSpec((PEER_HEADS, SUBLANE, tm), lambda j, i: (0, cur(i) // per_blk, j))
    spec2 = pl.BlockSpec((PEER_HEADS, PEER_N_KEYS, tm), lambda j, i: (0, 0, j), pipeline_mode=once)
    return pl.pallas_call(
        functools.partial(_peer_expert_kernel, krows=krows),
        grid=(pl.cdiv(n, tm), n_tiles + 1),
        in_specs=[
            pl.BlockSpec((tm, d), lambda j, i: (j, 0), pipeline_mode=once),
            pl.BlockSpec((te, d), lambda j, i: (cur(i), 0)),
            pl.BlockSpec((te, d), lambda j, i: (jnp.maximum(i - 1, 0), 0)),
            spec1, spec1, spec2, spec2,
        ],
        out_specs=pl.BlockSpec((tm, d), lambda j, i: (j, 0), pipeline_mode=once),
        out_shape=jax.ShapeDtypeStruct((n, d), F32),
        scratch_shapes=[pltpu.VMEM((2, te, tm), BF16)],
        compiler_params=_cparams(("parallel", "arbitrary")),
        name="peer_experts",
    )(xn, wd, wu, e1, ks, e2, rb)


def _rope_table(pos):
    inv = ROPE_THETA ** (-jnp.arange(0, ROPE_DIM, 2, dtype=F32) / ROPE_DIM)
    ang = pos.astype(F32)[:, None] * inv[None, :]
    reps = LANE // (ROPE_DIM // 2)
    return jnp.concatenate([jnp.tile(jnp.cos(ang), (1, reps)), jnp.tile(jnp.sin(ang), (1, reps))], axis=1)


def _rot_cols(w):
    half = ROPE_DIM // 2
    return jnp.concatenate([-w[..., half:], w[..., :half]], axis=-1)


def _pad_cols(w, width):
    return jnp.pad(w, [(0, 0)] * (w.ndim - 1) + [(0, width - w.shape[-1])])


def kernel(x_prompt, x_sample, p_prompt, p_sample, cache_kv_latent, cache_k_rope, state_mlstm_C, state_mlstm_n, state_mlstm_m, page_table, g_mix_norm, w_in, g_q_latent, w_uq, g_kv_latent, w_uk, w_uv, b_igate, b_fgate, g_mlstm_head, w_out, g_ffn_norm, peer_w_query, peer_sub_keys, peer_w_down, peer_w_up, g_ple_norm, w_ple, w_ple_gate, g_final_norm):
    B, S, D = x_prompt.shape
    DB, T, _ = x_sample.shape
    depth = w_in.shape[0]
    q_lora, kv_lora = g_q_latent.shape[1], g_kv_latent.shape[1]
    H, DK, DV = MLSTM_HEADS, MLSTM_QK_DIM, MLSTM_V_DIM
    n_p, n_s = B * S, DB * T
    n = n_p + n_s
    n_past = page_table.shape[1] * PAGE_SIZE
    scale = (NOPE_DIM + ROPE_DIM) ** -0.5
    t_pad = SUBLANE * ((T + SUBLANE - 1) // SUBLANE)

    zo, off = {}, 0
    for name, width in (("mq", H * DK), ("mk", H * DK), ("mv", H * DV), ("mo", H * DV), ("cq", q_lora),
                        ("ckv", kv_lora), ("kr", 2 * LANE), ("gate", LANE)):
        zo[name] = off
        off += width
    zw = 1024 * ((off + 1023) // 1024)

    cs = jnp.concatenate([jnp.tile(_rope_table(jnp.arange(S, dtype=I32)), (B, 1)),
                          jnp.tile(_rope_table(n_past + jnp.arange(T, dtype=I32)), (DB, 1))], axis=0)
    tmq = _tile(math.gcd(S, n_s), 128, BF16_SUBLANE)
    chunk_p = math.gcd(S, MLSTM_KERNEL_CHUNK)

    cache_kr_t = jnp.swapaxes(cache_k_rope, 2, 3)
    x = jnp.concatenate([x_prompt.reshape(n_p, D), x_sample.reshape(n_s, D)], axis=0)
    outs = {k: [] for k in ("ckv_p", "kr_p", "ckv_s", "kr_s", "C_p", "n_p", "m_p", "C_s", "n_s", "m_s")}
    for l in range(depth):
        sizes = (q_lora, kv_lora, ROPE_DIM, H * DK, H * DK, H * DV, H * DV, H, H)
        parts, st = [], 0
        for sz in sizes:
            parts.append(w_in[l][:, st:st + sz])
            st += sz
        wcq, wckv, wkr, wmq, wmk, wmv, wmo, wmi, wmf = parts
        w_z = jnp.concatenate([wmq, wmk, wmv, wmo, wcq, wckv, _pad_cols(wkr, LANE), _pad_cols(_rot_cols(wkr), LANE),
                               wmi, wmf], axis=1)
        w_z = _pad_cols(w_z, zw).astype(BF16)
        wq3 = w_uq[l].reshape(q_lora, MLA_HEADS, NOPE_DIM + ROPE_DIM)
        wn = wq3[:, :, :NOPE_DIM].reshape(q_lora, MLA_HEADS * NOPE_DIM).astype(BF16)
        wr = _pad_cols(wq3[:, :, NOPE_DIM:], LANE).reshape(q_lora, MLA_HEADS * LANE).astype(BF16)
        wrr = _pad_cols(_rot_cols(wq3[:, :, NOPE_DIM:]), LANE).reshape(q_lora, MLA_HEADS * LANE).astype(BF16)
        wuk = w_uk[l].reshape(kv_lora, MLA_HEADS, NOPE_DIM).transpose(1, 2, 0).astype(BF16)
        wuv = w_uv[l].reshape(kv_lora, MLA_HEADS, MLA_V_DIM).transpose(1, 0, 2).astype(BF16)
        bias = jnp.stack([b_igate[l], b_fgate[l]]).astype(F32)

        hn = _rmsnorm(x, g_mix_norm[l], BF16)
        z = _matmul(hn, w_z)

        q4, ckv, kr, kvb = _qprep(z, cs, g_q_latent[l], g_kv_latent[l], wn, wr, wrr, wuk, tmq, zo)
        o_mla_p = _attn_prompt(q4, kvb, wuv, B, S, tmq, scale)
        nblk_s = n_s // tmq
        qs = q4[n_p // tmq:].transpose(1, 0, 2, 3).reshape(MLA_HEADS, DB, T, KV_W).transpose(1, 0, 2, 3)
        qs = jnp.pad(qs, ((0, 0), (0, 0), (0, t_pad - T), (0, 0))).reshape(DB, MLA_HEADS * t_pad, KV_W)
        kvn = jnp.pad(kvb[n_p:].reshape(DB, T, KV_W), ((0, 0), (0, t_pad - T), (0, 0)))
        o_mla_s = _attn_sample(page_table, qs, kvn, wuv, cache_kv_latent, cache_kr_t, l, scale, T)
        o_mla = jnp.concatenate([o_mla_p, o_mla_s[:, :T].reshape(n_s, -1).astype(BF16)], axis=0)

        gz = z[:, zo["gate"]:zo["gate"] + 2 * H]
        c0p = jnp.zeros((B, H, DK, DV), F32)
        n0p = jnp.zeros((B, H, DK), F32)
        m0p = jnp.full((B, H), M_EMPTY, F32)
        hg_p, C_p, nn_p, mm_p = _mlstm(z, gz[:n_p].reshape(B, S, 2, H), bias, g_mlstm_head[l], c0p, n0p, m0p,
                                       B, S, chunk_p, chunk_p, MLSTM_HEADS_PER_STEP_PROMPT, zo)
        z_s = jnp.pad(z[n_p:].reshape(DB, T, zw), ((0, 0), (0, t_pad - T), (0, 0))).reshape(DB * t_pad, zw)
        g_s = jnp.pad(gz[n_p:].reshape(DB, T, 2, H), ((0, 0), (0, t_pad - T), (0, 0), (0, 0)))
        hg_s, C_s, nn_s, mm_s = _mlstm(z_s, g_s, bias, g_mlstm_head[l], state_mlstm_C[l].astype(F32),
                                       state_mlstm_n[l].astype(F32), state_mlstm_m[l], DB, t_pad, t_pad, T,
                                       MLSTM_HEADS_PER_STEP_SAMPLE, zo)
        o_ml = jnp.concatenate([hg_p, hg_s.reshape(DB, t_pad, H * DV)[:, :T].reshape(n_s, H * DV)], axis=0)

        x1 = _outproj(o_mla, o_ml, w_out[l].astype(BF16), x)

        xn = _rmsnorm(x1, g_ffn_norm[l], BF16)
        pq = _matmul(xn, peer_w_query[l].astype(BF16))
        keys = peer_sub_keys[l].reshape(2 * PEER_HEADS, PEER_N_KEYS, -1).astype(BF16)
        tms = _tile(n, 256, LANE)
        sel = _peer_select(pq, keys, tms)
        tmx = min(PEER_TOKEN_TILE, LANE * pl.cdiv(n, LANE))
        peer_out = _peer_experts(xn, peer_w_down[l].astype(BF16), peer_w_up[l].astype(BF16), sel, tmx, PEER_EXPERT_TILE)

        x2, xg = _add_rmsnorm(x1, peer_out, g_ple_norm[l])
        pcat = jnp.concatenate([p_prompt[l].reshape(n_p, -1), p_sample[l].reshape(n_s, -1)], axis=0).astype(BF16)
        x = _ple(xg, w_ple_gate[l].astype(BF16), pcat, w_ple[l].astype(BF16), x2)

        outs["ckv_p"].append(ckv[:n_p].reshape(B, S, kv_lora))
        outs["kr_p"].append(kr[:n_p].reshape(B, S, ROPE_DIM))
        outs["ckv_s"].append(ckv[n_p:].reshape(DB, T, kv_lora))
        outs["kr_s"].append(kr[n_p:].reshape(DB, T, ROPE_DIM))
        outs["C_p"].append(C_p)
        outs["n_p"].append(nn_p)
        outs["m_p"].append(mm_p)
        outs["C_s"].append(C_s.astype(state_mlstm_C.dtype))
        outs["n_s"].append(nn_s.astype(state_mlstm_n.dtype))
        outs["m_s"].append(mm_s.astype(state_mlstm_m.dtype))

    y_p = _rmsnorm(x, g_final_norm, F32, 0, n_p)
    y_s = _rmsnorm(x, g_final_norm, F32, n_p, n_s)
    st = {k: jnp.stack(v) for k, v in outs.items()}
    return (y_p.reshape(B, S, D), y_s.reshape(DB, T, D), st["ckv_p"], st["kr_p"], st["ckv_s"], st["kr_s"],
            st["C_p"], st["n_p"], st["m_p"], st["C_s"], st["n_s"], st["m_s"])
```

```python
import functools
import math

import jax
import jax.numpy as jnp
from jax import lax
from jax.experimental import pallas as pl
from jax.experimental.pallas import tpu as pltpu

F32, BF16, I32 = jnp.float32, jnp.bfloat16, jnp.int32

MLA_HEADS = 16
NOPE_DIM = 128
ROPE_DIM = 64
MLA_V_DIM = 128
ROPE_THETA = 10000.0
MLSTM_HEADS = 8
MLSTM_QK_DIM = 128
MLSTM_V_DIM = 256
MLSTM_CHUNK = 64
M_EMPTY = -1e30
PEER_HEADS = 8
PEER_N_KEYS = 128
PEER_TOPK = 16
PAGE_SIZE = 128
EPS = 1e-6
INV_SQRT2 = 0.7071067811865476

LANE = 128
SUBLANE = 8
BF16_SUBLANE = 16
VMEM_LIMIT_BYTES = 60 * 1024 * 1024

KV_W = 640
NOT_A_RANK = 1.0e9
ORDER_MAX = float(2 ** 30)
CAND_ROW_BLOCKS = 4
MLSTM_KERNEL_CHUNK = 128
PEER_TOKEN_TILE = 1024
PEER_EXPERT_TILE = 512
MLSTM_HEADS_PER_STEP_PROMPT = 2
MLSTM_HEADS_PER_STEP_SAMPLE = 8


def _tile(n, cap, mult):
    best = None
    d = mult
    while d <= min(n, cap):
        if n % d == 0:
            best = d
        d += mult
    return n if best is None else best


def _cparams(sem):
    return pltpu.CompilerParams(dimension_semantics=sem, vmem_limit_bytes=VMEM_LIMIT_BYTES)


def _dot(a, b):
    return jnp.dot(a, b, preferred_element_type=F32)


def _dot_nt(a, b):
    return lax.dot_general(a, b, (((1,), (1,)), ((), ())), preferred_element_type=F32)


def _dot_tn(a, b):
    return lax.dot_general(a, b, (((0,), (0,)), ((), ())), preferred_element_type=F32)


def _rms(x, g):
    return x * lax.rsqrt(jnp.mean(x * x, axis=-1, keepdims=True) + EPS) * g


def _norm_kernel(x_ref, g_ref, o_ref):
    o_ref[...] = _rms(x_ref[...], g_ref[...]).astype(o_ref.dtype)


def _rmsnorm(x, g, out_dtype, row0=0, nrows=None, pad_to=None):
    d = x.shape[1]
    n = x.shape[0] if nrows is None else nrows
    n_out = n if pad_to is None else pad_to
    tm = _tile(math.gcd(math.gcd(row0, n), n_out), 256, BF16_SUBLANE)
    first, last = row0 // tm, row0 // tm + n // tm - 1
    return pl.pallas_call(
        _norm_kernel,
        grid=(n_out // tm,),
        in_specs=[pl.BlockSpec((tm, d), lambda i: (jnp.minimum(first + i, last), 0)),
                  pl.BlockSpec((1, d), lambda i: (0, 0))],
        out_specs=pl.BlockSpec((tm, d), lambda i: (i, 0)),
        out_shape=jax.ShapeDtypeStruct((n_out, d), out_dtype),
        compiler_params=_cparams(("parallel",)),
        name="rmsnorm",
    )(x, g.reshape(1, d))


def _add_norm_kernel(a_ref, b_ref, g_ref, s_ref, o_ref):
    s = a_ref[...] + b_ref[...]
    s_ref[...] = s
    o_ref[...] = _rms(s, g_ref[...]).astype(o_ref.dtype)


def _add_rmsnorm(a, b, g):
    n, d = a.shape
    tm = _tile(n, 128, BF16_SUBLANE)
    row = pl.BlockSpec((tm, d), lambda i: (i, 0))
    return pl.pallas_call(
        _add_norm_kernel,
        grid=(n // tm,),
        in_specs=[row, row, pl.BlockSpec((1, d), lambda i: (0, 0))],
        out_specs=[row, row],
        out_shape=[jax.ShapeDtypeStruct((n, d), F32), jax.ShapeDtypeStruct((n, d), BF16)],
        compiler_params=_cparams(("parallel",)),
        name="add_rmsnorm",
    )(a, b, g.reshape(1, d))


def _mm_kernel(a_ref, b_ref, o_ref):
    o_ref[...] = _dot(a_ref[...], b_ref[...]).astype(o_ref.dtype)


def _matmul(a, b, out_dtype=F32):
    m, k = a.shape
    n = b.shape[1]
    tm = _tile(m, 1088, BF16_SUBLANE)
    tn = _tile(n, 1024, LANE)
    return pl.pallas_call(
        _mm_kernel,
        grid=(m // tm, n // tn),
        in_specs=[pl.BlockSpec((tm, k), lambda i, j: (i, 0)), pl.BlockSpec((k, tn), lambda i, j: (0, j))],
        out_specs=pl.BlockSpec((tm, tn), lambda i, j: (i, j)),
        out_shape=jax.ShapeDtypeStruct((m, n), out_dtype),
        compiler_params=_cparams(("parallel", "parallel")),
        name="matmul",
    )(a, b)


def _outproj_kernel(a1_ref, a2_ref, b1_ref, b2_ref, r_ref, o_ref):
    o_ref[...] = r_ref[...] + (_dot(a1_ref[...], b1_ref[...]) + _dot(a2_ref[...], b2_ref[...]))


def _outproj(a1, a2, w, r):
    m, k1 = a1.shape
    k2 = a2.shape[1]
    n = w.shape[1]
    tm = _tile(m, 1088, BF16_SUBLANE)
    tn = _tile(n, 512, LANE)
    nb1 = k1 // k2
    return pl.pallas_call(
        _outproj_kernel,
        grid=(m // tm, n // tn),
        in_specs=[
            pl.BlockSpec((tm, k1), lambda i, j: (i, 0)),
            pl.BlockSpec((tm, k2), lambda i, j: (i, 0)),
            pl.BlockSpec((k1, tn), lambda i, j: (0, j)),
            pl.BlockSpec((k2, tn), lambda i, j: (nb1, j)),
            pl.BlockSpec((tm, tn), lambda i, j: (i, j)),
        ],
        out_specs=pl.BlockSpec((tm, tn), lambda i, j: (i, j)),
        out_shape=jax.ShapeDtypeStruct((m, n), F32),
        compiler_params=_cparams(("parallel", "parallel")),
        name="outproj",
    )(a1, a2, w, w, r)


def _ple_kernel(xg_ref, wg_ref, p_ref, wp_ref, x_ref, o_ref):
    gate = _dot(xg_ref[...], wg_ref[...])
    o_ref[...] = x_ref[...] + _dot(p_ref[...], wp_ref[...]) * jax.nn.sigmoid(gate)


def _ple(xg, w_gate, p, w_ple, x):
    m, k = xg.shape
    n = w_gate.shape[1]
    kp = p.shape[1]
    tm = _tile(m, 1088, BF16_SUBLANE)
    tn = _tile(n, 512, LANE)
    return pl.pallas_call(
        _ple_kernel,
        grid=(m // tm, n // tn),
        in_specs=[
            pl.BlockSpec((tm, k), lambda i, j: (i, 0)),
            pl.BlockSpec((k, tn), lambda i, j: (0, j)),
            pl.BlockSpec((tm, kp), lambda i, j: (i, 0)),
            pl.BlockSpec((kp, tn), lambda i, j: (0, j)),
            pl.BlockSpec((tm, tn), lambda i, j: (i, j)),
        ],
        out_specs=pl.BlockSpec((tm, tn), lambda i, j: (i, j)),
        out_shape=jax.ShapeDtypeStruct((m, n), F32),
        compiler_params=_cparams(("parallel", "parallel")),
        name="ple",
    )(xg, w_gate, p, w_ple, x)


def _qprep_kernel(cq_ref, ckv_ref, krk_ref, cs_ref, gq_ref, gkv_ref, wn_ref, wr_ref, wrr_ref, wuk_ref,
                  q_ref, ckv_out_ref, kr_out_ref, kv_ref):
    cqn = _rms(cq_ref[...], gq_ref[...]).astype(BF16)
    cs = cs_ref[...]
    cos1, sin1 = cs[:, :LANE], cs[:, LANE:]
    cos_t = jnp.concatenate([cos1] * MLA_HEADS, axis=1)
    sin_t = jnp.concatenate([sin1] * MLA_HEADS, axis=1)
    qn = _dot(cqn, wn_ref[...])
    qr = _dot(cqn, wr_ref[...]) * cos_t + _dot(cqn, wrr_ref[...]) * sin_t
    for h in range(MLA_HEADS):
        sl = slice(h * LANE, (h + 1) * LANE)
        qa = _dot(qn[:, sl].astype(BF16), wuk_ref[h])
        q_ref[0, h, :, 0:512] = qa.astype(BF16)
        q_ref[0, h, :, 512:KV_W] = qr[:, sl].astype(BF16)
    c = _rms(ckv_ref[...], gkv_ref[...])
    ckv_out_ref[...] = c
    krk = krk_ref[...]
    kr = krk[:, :LANE] * cos1 + krk[:, LANE:] * sin1
    kr_out_ref[...] = kr[:, :ROPE_DIM]
    kv_ref[:, 0:512] = c.astype(BF16)
    kv_ref[:, 512:KV_W] = kr.astype(BF16)


def _qprep(z, cs, g_q, g_kv, wn, wr, wrr, wuk, tm, zo):
    n = z.shape[0]
    q_lora = wn.shape[0]
    kv_lora = wuk.shape[2]
    const2 = lambda i: (0, 0)
    return pl.pallas_call(
        _qprep_kernel,
        grid=(n // tm,),
        in_specs=[
            pl.BlockSpec((tm, q_lora), lambda i: (i, zo["cq"] // q_lora)),
            pl.BlockSpec((tm, kv_lora), lambda i: (i, zo["ckv"] // kv_lora)),
            pl.BlockSpec((tm, 2 * LANE), lambda i: (i, zo["kr"] // (2 * LANE))),
            pl.BlockSpec((tm, 2 * LANE), lambda i: (i, 0)),
            pl.BlockSpec((1, q_lora), const2),
            pl.BlockSpec((1, kv_lora), const2),
            pl.BlockSpec(wn.shape, const2),
            pl.BlockSpec(wr.shape, const2),
            pl.BlockSpec(wrr.shape, const2),
            pl.BlockSpec(wuk.shape, lambda i: (0, 0, 0)),
        ],
        out_specs=[
            pl.BlockSpec((1, MLA_HEADS, tm, KV_W), lambda i: (i, 0, 0, 0)),
            pl.BlockSpec((tm, kv_lora), lambda i: (i, 0)),
            pl.BlockSpec((tm, ROPE_DIM), lambda i: (i, 0)),
            pl.BlockSpec((tm, KV_W), lambda i: (i, 0)),
        ],
        out_shape=[
            jax.ShapeDtypeStruct((n // tm, MLA_HEADS, tm, KV_W), BF16),
            jax.ShapeDtypeStruct((n, kv_lora), F32),
            jax.ShapeDtypeStruct((n, ROPE_DIM), F32),
            jax.ShapeDtypeStruct((n, KV_W), BF16),
        ],
        compiler_params=_cparams(("parallel",)),
        name="mla_qprep",
    )(z, z, z, cs, g_q.reshape(1, -1), g_kv.reshape(1, -1), wn, wr, wrr, wuk)


def _softmax_step(s, v, m_sc, l_sc, acc_sc):
    m_old = m_sc[...]
    m_new = jnp.maximum(m_old, jnp.max(s, axis=-1, keepdims=True))
    alpha = jnp.exp(m_old - m_new)
    p = jnp.exp(s - m_new)
    l_sc[...] = alpha * l_sc[...] + jnp.sum(p, axis=-1, keepdims=True)
    acc_sc[...] = alpha * acc_sc[...] + _dot(p.astype(BF16), v)
    m_sc[...] = m_new


def _attn_prompt_kernel(q_ref, kv_ref, wuv_ref, o_ref, m_sc, l_sc, acc_sc, s_sc, *, tq, tk, scale, kv_lora):
    qi = pl.program_id(1)
    rows = MLA_HEADS * tq
    q = q_ref[0].reshape(rows, KV_W)
    m_sc[...] = jnp.full(m_sc.shape, -jnp.inf, F32)
    l_sc[...] = jnp.zeros(l_sc.shape, F32)
    acc_sc[...] = jnp.zeros(acc_sc.shape, F32)
    nfull = (qi * tq) // tk

    def keys(j):
        return kv_ref[pl.ds(pl.multiple_of(j * tk, tk), tk), :]

    s_sc[0] = _dot_nt(q, keys(0)) * scale

    def body(j, carry):
        s = s_sc[lax.rem(j, 2)]
        s_sc[lax.rem(j + 1, 2)] = _dot_nt(q, keys(j + 1)) * scale
        _softmax_step(s, keys(j)[:, :kv_lora], m_sc, l_sc, acc_sc)
        return carry

    lax.fori_loop(0, nfull, body, 0)
    qpos = qi * tq + lax.rem(lax.broadcasted_iota(I32, (rows, 1), 0), tq)
    kpos = nfull * tk + lax.broadcasted_iota(I32, (1, tk), 1)
    s = jnp.where(kpos <= qpos, s_sc[lax.rem(nfull, 2)], -jnp.inf)
    _softmax_step(s, keys(nfull)[:, :kv_lora], m_sc, l_sc, acc_sc)
    o = acc_sc[...] / l_sc[...]
    for h in range(MLA_HEADS):
        oh = o[h * tq:(h + 1) * tq].astype(BF16)
        o_ref[:, h * MLA_V_DIM:(h + 1) * MLA_V_DIM] = _dot(oh, wuv_ref[h]).astype(o_ref.dtype)


def _attn_prompt(q4, kv, wuv, batch, seq, tq, scale):
    kv_lora = wuv.shape[1]
    nq = seq // tq
    tk = _tile(seq, 512, tq)
    rows = MLA_HEADS * tq
    kern = functools.partial(_attn_prompt_kernel, tq=tq, tk=tk, scale=scale, kv_lora=kv_lora)
    return pl.pallas_call(
        kern,
        grid=(batch, nq),
        in_specs=[
            pl.BlockSpec((1, MLA_HEADS, tq, KV_W), lambda b, i: (b * nq + i, 0, 0, 0)),
            pl.BlockSpec((seq, KV_W), lambda b, i: (b, 0)),
            pl.BlockSpec(wuv.shape, lambda b, i: (0, 0, 0)),
        ],
        out_specs=pl.BlockSpec((tq, MLA_HEADS * MLA_V_DIM), lambda b, i: (b * nq + i, 0)),
        out_shape=jax.ShapeDtypeStruct((batch * seq, MLA_HEADS * MLA_V_DIM), BF16),
        scratch_shapes=[pltpu.VMEM((rows, 1), F32), pltpu.VMEM((rows, 1), F32), pltpu.VMEM((rows, kv_lora), F32),
                        pltpu.VMEM((2, rows, tk), F32)],
        compiler_params=_cparams(("parallel", "arbitrary")),
        name="mla_prompt_attention",
    )(q4, kv, wuv)


def _page_copies(pt_ref, ckv_hbm, ckr_hbm, cin, rin, sem_c, sem_r, seq, chunk, slot, *, pp, layer):
    page = ckv_hbm.shape[2]
    copies = []
    for i in range(pp):
        pid = pt_ref[seq, chunk * pp + i]
        copies.append(pltpu.make_async_copy(ckv_hbm.at[layer, pid], cin.at[slot, pl.ds(i * page, page)], sem_c.at[slot]))
        copies.append(pltpu.make_async_copy(ckr_hbm.at[layer, pid], rin.at[slot, i], sem_r.at[slot]))
    return copies


def _attn_sample_kernel(pt_ref, q_ref, kvn_ref, wuv_ref, ckv_hbm, ckr_hbm, o_ref, cin, rin, sem_c, sem_r, cbuf, rbuf,
                        m_sc, l_sc, acc_sc, *, pp, layer, scale, t_valid, t_pad, kv_lora):
    s_id, c = pl.program_id(0), pl.program_id(1)
    nchunk = pl.num_programs(1)
    k = s_id * nchunk + c
    slot = lax.rem(k, 2)
    page = ckv_hbm.shape[2]
    copies = functools.partial(_page_copies, pt_ref, ckv_hbm, ckr_hbm, cin, rin, sem_c, sem_r, pp=pp, layer=layer)

    @pl.when(k == 0)
    def _():
        for j, cp in enumerate(copies(s_id, c, slot)):
            cp.start(priority=(j // 2) % 2)

    @pl.when(k + 1 < pl.num_programs(0) * nchunk)
    def _():
        wrap = c + 1 == nchunk
        nxt_s = jnp.where(wrap, s_id + 1, s_id)
        nxt_c = jnp.where(wrap, 0, c + 1)
        for j, cp in enumerate(copies(nxt_s, nxt_c, 1 - slot)):
            cp.start(priority=(j // 2) % 2)

    @pl.when(c == 0)
    def _():
        m_sc[...] = jnp.full(m_sc.shape, -jnp.inf, F32)
        l_sc[...] = jnp.zeros(l_sc.shape, F32)
        acc_sc[...] = jnp.zeros(acc_sc.shape, F32)

    for cp in copies(s_id, c, slot):
        cp.wait()
    for i in range(pp):
        cbuf[i * page:(i + 1) * page, :] = cin[slot, pl.ds(i * page, page), :].astype(BF16)
        rbuf[:, i * page:(i + 1) * page] = rin[slot, i].astype(BF16)
    q = q_ref[0]
    kc = cbuf[...]
    s = _dot_nt(q[:, :kv_lora], kc) + _dot(q[:, kv_lora:kv_lora + ROPE_DIM], rbuf[...])
    _softmax_step(s * scale, kc, m_sc, l_sc, acc_sc)

    @pl.when(c == pl.num_programs(1) - 1)
    def _():
        kn = kvn_ref[0]
        s = _dot_nt(q, kn) * scale
        qt = lax.rem(lax.broadcasted_iota(I32, (q.shape[0], 1), 0), t_pad)
        kt = lax.broadcasted_iota(I32, (1, t_pad), 1)
        s = jnp.where((kt <= qt) & (kt < t_valid), s, -jnp.inf)
        _softmax_step(s, kn[:, :kv_lora], m_sc, l_sc, acc_sc)
        o = acc_sc[...] / l_sc[...]
        for h in range(MLA_HEADS):
            oh = o[h * t_pad:(h + 1) * t_pad].astype(BF16)
            o_ref[0, :, h * MLA_V_DIM:(h + 1) * MLA_V_DIM] = _dot(oh, wuv_ref[h])


def _attn_sample(page_table, qs, kvn, wuv, cache_kv, cache_kr_t, layer, scale, t_valid):
    nseq, rows, _ = qs.shape
    t_pad = kvn.shape[1]
    npages = page_table.shape[1]
    page, kv_lora = cache_kv.shape[2], cache_kv.shape[3]
    pp = _tile(npages, 32, 1)
    kern = functools.partial(_attn_sample_kernel, pp=pp, layer=layer, scale=scale, t_valid=t_valid, t_pad=t_pad,
                             kv_lora=kv_lora)
    grid_spec = pltpu.PrefetchScalarGridSpec(
        num_scalar_prefetch=1,
        grid=(nseq, npages // pp),
        in_specs=[
            pl.BlockSpec((1, rows, KV_W), lambda s, c, pt: (s, 0, 0)),
            pl.BlockSpec((1, t_pad, KV_W), lambda s, c, pt: (s, 0, 0)),
            pl.BlockSpec(wuv.shape, lambda s, c, pt: (0, 0, 0)),
            pl.BlockSpec(memory_space=pl.ANY),
            pl.BlockSpec(memory_space=pl.ANY),
        ],
        out_specs=pl.BlockSpec((1, t_pad, MLA_HEADS * MLA_V_DIM), lambda s, c, pt: (s, 0, 0)),
        scratch_shapes=[
            pltpu.VMEM((2, pp * page, kv_lora), F32),
            pltpu.VMEM((2, pp, ROPE_DIM, page), F32),
            pltpu.SemaphoreType.DMA((2,)),
            pltpu.SemaphoreType.DMA((2,)),
            pltpu.VMEM((pp * page, kv_lora), BF16),
            pltpu.VMEM((ROPE_DIM, pp * page), BF16),
            pltpu.VMEM((rows, 1), F32),
            pltpu.VMEM((rows, 1), F32),
            pltpu.VMEM((rows, kv_lora), F32),
        ],
    )
    return pl.pallas_call(
        kern,
        grid_spec=grid_spec,
        out_shape=jax.ShapeDtypeStruct((nseq, t_pad, MLA_HEADS * MLA_V_DIM), F32),
        compiler_params=_cparams(("arbitrary", "arbitrary")),
        name="mla_sample_attention",
    )(page_table, qs, kvn, wuv, cache_kv, cache_kr_t)


def _mlstm_kernel(bias_ref, q_ref, k_ref, v_ref, mo_ref, gcol_ref, grow_ref, gmh_ref, c0_ref, n0_ref, m0_ref,
                  h_ref, c_ref, n_ref, m_ref, c_sc, n_sc, m_sc, *, chunk, nchunks, valid, hps):
    DK, DV = MLSTM_QK_DIM, MLSTM_V_DIM
    hg = pl.program_id(1)
    c_sc[...] = c0_ref[0]
    n_sc[...] = n0_ref[0]
    m_sc[...] = m0_ref[0]
    L = chunk
    li = lax.broadcasted_iota(I32, (L, L), 0)
    si = lax.broadcasted_iota(I32, (L, L), 1)
    causal = si <= li
    col_ok = lax.broadcasted_iota(I32, (L, 1), 0) < valid
    row_ok = lax.broadcasted_iota(I32, (1, L), 1) < valid
    kscale = MLSTM_QK_DIM ** -0.5

    def head_step(c, r0, hh_):
        qk = slice(hh_ * DK, (hh_ + 1) * DK)
        vv = slice(hh_ * DV, (hh_ + 1) * DV)
        b_i, b_f = bias_ref[0, hg * hps + hh_], bias_ref[1, hg * hps + hh_]
        q = q_ref[pl.ds(r0, L), qk]
        ks = k_ref[pl.ds(r0, L), qk] * kscale
        v = v_ref[pl.ds(r0, L), vv].astype(BF16)
        gc = gcol_ref[pl.ds(r0, L), 2 * hh_:2 * hh_ + 2]
        gr = grow_ref[0, hh_, pl.ds(c, 1), :]
        ig_col = jnp.where(col_ok, gc[:, 0:1] + b_i, -jnp.inf)
        ig_row = jnp.where(row_ok, gr[:, 0:L] + b_i, -jnp.inf)
        fl_col = jnp.where(col_ok, jax.nn.log_sigmoid(gc[:, 1:2] + b_f), 0.0)
        fl_row = jnp.where(row_ok, jax.nn.log_sigmoid(gr[:, L:2 * L] + b_f), 0.0)
        b_col = jnp.sum(jnp.where(causal, fl_row, 0.0), axis=1, keepdims=True)
        b_row = jnp.sum(jnp.where(li <= si, fl_col, 0.0), axis=0, keepdims=True)
        m_prev = m_sc[hh_:hh_ + 1, 0:1]
        n_prev = n_sc[hh_:hh_ + 1, :]
        c_prev = c_sc[hh_]
        dmat = jnp.where(causal, b_col - b_row + ig_row, -jnp.inf)
        m_inter = b_col + m_prev
        m_t = jnp.maximum(m_inter, jnp.max(dmat, axis=1, keepdims=True))
        w_inter = jnp.exp(m_inter - m_t)
        qb = q.astype(BF16)
        s = _dot_nt(qb, ks.astype(BF16)) * jnp.exp(dmat - m_t)
        num = w_inter * _dot(qb, c_prev.astype(BF16)) + _dot(s.astype(BF16), v)
        den = w_inter * jnp.sum(q * n_prev, axis=1, keepdims=True) + jnp.sum(s, axis=1, keepdims=True)
        hh = num / jnp.maximum(jnp.abs(den), jnp.exp(-m_t))
        y = _rms(hh, gmh_ref[:, vv])
        h_ref[pl.ds(r0, L), vv] = (y * jax.nn.sigmoid(mo_ref[pl.ds(r0, L), vv])).astype(h_ref.dtype)
        b_last = jnp.sum(fl_row, axis=1, keepdims=True)
        w_end_col = b_last - b_col + ig_col
        w_end_row = b_last - b_row + ig_row
        m_new = jnp.maximum(b_last + m_prev, jnp.max(w_end_row, axis=1, keepdims=True))
        a_state = jnp.exp(b_last + m_prev - m_new)
        ka = ks * jnp.exp(w_end_col - m_new)
        c_sc[hh_] = a_state * c_prev + _dot_tn(ka.astype(BF16), v)
        n_sc[hh_:hh_ + 1, :] = a_state * n_prev + jnp.sum(ka, axis=0, keepdims=True)
        m_sc[hh_:hh_ + 1, :] = jnp.broadcast_to(m_new, (1, m_sc.shape[1]))

    def step(c, carry):
        r0 = pl.multiple_of(c * L, L)
        for hh_ in range(hps):
            head_step(c, r0, hh_)
        return carry

    lax.fori_loop(0, nchunks, step, 0)
    c_ref[0] = c_sc[...]
    n_ref[0] = n_sc[...]
    m_ref[0] = m_sc[...]


def _mlstm(z2, gates, bias, g_mh, c0, n0, m0, batch, t_rows, chunk, valid, hps, zo):
    H, DK, DV = MLSTM_HEADS, MLSTM_QK_DIM, MLSTM_V_DIM
    nchunks = t_rows // chunk
    ng = H // hps
    g6 = gates.reshape(batch, t_rows, 2, ng, hps)
    gcol = g6.transpose(0, 3, 1, 4, 2).reshape(batch * ng * t_rows, 2 * hps)
    grow = g6.reshape(batch, nchunks, chunk, 2, ng, hps).transpose(0, 4, 5, 1, 3, 2)
    grow = grow.reshape(batch * ng, hps, nchunks, 2 * chunk)
    c0r = c0.reshape(batch * ng, hps, DK, DV)
    n0r = n0.reshape(batch * ng, hps, DK)
    m0r = jnp.broadcast_to(m0.reshape(batch * ng, hps, 1).astype(F32), (batch * ng, hps, LANE))
    kern = functools.partial(_mlstm_kernel, chunk=chunk, nchunks=nchunks, valid=valid, hps=hps)
    qk_w, v_w = hps * DK, hps * DV
    grp3 = lambda b, g, bias: (b * ng + g, 0, 0)
    grp4 = lambda b, g, bias: (b * ng + g, 0, 0, 0)
    grid_spec = pltpu.PrefetchScalarGridSpec(
        num_scalar_prefetch=1,
        grid=(batch, ng),
        in_specs=[
            pl.BlockSpec((t_rows, qk_w), lambda b, g, bias: (b, zo["mq"] // qk_w + g)),
            pl.BlockSpec((t_rows, qk_w), lambda b, g, bias: (b, zo["mk"] // qk_w + g)),
            pl.BlockSpec((t_rows, v_w), lambda b, g, bias: (b, zo["mv"] // v_w + g)),
            pl.BlockSpec((t_rows, v_w), lambda b, g, bias: (b, zo["mo"] // v_w + g)),
            pl.BlockSpec((t_rows, 2 * hps), lambda b, g, bias: (b * ng + g, 0)),
            pl.BlockSpec((1, hps, nchunks, 2 * chunk), grp4),
            pl.BlockSpec((1, v_w), lambda b, g, bias: (0, g)),
            pl.BlockSpec((1, hps, DK, DV), grp4),
            pl.BlockSpec((1, hps, DK), grp3),
            pl.BlockSpec((1, hps, LANE), grp3),
        ],
        out_specs=[
            pl.BlockSpec((t_rows, v_w), lambda b, g, bias: (b, g)),
            pl.BlockSpec((1, hps, DK, DV), grp4),
            pl.BlockSpec((1, hps, DK), grp3),
            pl.BlockSpec((1, hps, LANE), grp3),
        ],
        scratch_shapes=[pltpu.VMEM((hps, DK, DV), F32), pltpu.VMEM((hps, DK), F32), pltpu.VMEM((hps, LANE), F32)],
    )
    hg, c, n, m = pl.pallas_call(
        kern,
        grid_spec=grid_spec,
        out_shape=[
            jax.ShapeDtypeStruct((batch * t_rows, H * DV), BF16),
            jax.ShapeDtypeStruct((batch * ng, hps, DK, DV), F32),
            jax.ShapeDtypeStruct((batch * ng, hps, DK), F32),
            jax.ShapeDtypeStruct((batch * ng, hps, LANE), F32),
        ],
        compiler_params=_cparams(("parallel", "parallel")),
        name="mlstm",
    )(bias, z2, z2, z2, z2, gcol, grow, g_mh.reshape(1, H * DV), c0r, n0r, m0r)
    return hg, c.reshape(batch, H, DK, DV), n.reshape(batch, H, DK), m[:, :, 0].reshape(batch, H)


def _take_top(work, order, nsel, want_rank):
    rank = jnp.full(work.shape, NOT_A_RANK, F32) if want_rank else None
    vals, picks = [], []
    for a in range(nsel):
        m = jnp.max(work, axis=0, keepdims=True)
        am = jnp.min(jnp.where(work == m, order, ORDER_MAX), axis=0, keepdims=True)
        hit = order == am
        vals.append(m)
        picks.append(am)
        if want_rank:
            rank = jnp.where(hit, float(a), rank)
        work = jnp.where(hit, -jnp.inf, work)
    return vals, picks, rank


def _candidate_blocks(tm):
    K = PEER_TOPK
    blocks, pos, valid = [], [], []
    for a in range(CAND_ROW_BLOCKS):
        nb = K // (a + 1)
        rows = SUBLANE * ((nb + SUBLANE - 1) // SUBLANE)
        it = lax.broadcasted_iota(I32, (rows, tm), 0)
        blocks.append(("row", a, rows))
        pos.append(a * K + it)
        valid.append(it < nb)
    for b in range(K):
        na = K // (b + 1)
        if na <= CAND_ROW_BLOCKS:
            break
        rows = SUBLANE * ((na + SUBLANE - 1) // SUBLANE)
        it = lax.broadcasted_iota(I32, (rows, tm), 0)
        blocks.append(("col", b, rows))
        pos.append(it * K + b)
        valid.append((it >= CAND_ROW_BLOCKS) & (it < na))
    return blocks, jnp.concatenate(pos, axis=0).astype(F32), jnp.concatenate(valid, axis=0)


def _peer_select_kernel(pq_ref, keys_ref, e1_ref, ks_ref, e2_ref, rb_ref):
    K = PEER_TOPK
    tm = pq_ref.shape[0]
    key_row = lax.broadcasted_iota(I32, (PEER_N_KEYS, tm), 0).astype(F32)
    blocks, cand_pos, cand_ok = _candidate_blocks(tm)
    a_row = lax.broadcasted_iota(I32, (K, tm), 0)
    for h in range(PEER_HEADS):
        sides = []
        for p in range(2):
            qs = pq_ref[:, (2 * h + p) * LANE:(2 * h + p + 1) * LANE].astype(BF16)
            st = _dot_nt(keys_ref[2 * h + p], qs)
            vals, _, rank = _take_top(st, key_row, K, True)
            sides.append((st, vals, rank))
        (s1, v1, ra), (s2, v2, rb) = sides
        v1all = jnp.concatenate(v1, axis=0)
        v2all = jnp.concatenate(v2, axis=0)
        parts = [(v1[i] + v2all[:rows]) if kind == "row" else (v1all[:rows] + v2[i]) for kind, i, rows in blocks]
        cand = jnp.where(cand_ok, jnp.concatenate(parts, axis=0), -jnp.inf)
        top, pos, _ = _take_top(cand, cand_pos, K, False)
        zsum = jnp.ones_like(top[0])
        for k in range(1, K):
            zsum = zsum + jnp.exp(top[k] - top[0])
        count = jnp.zeros((K, tm), F32)
        for k in range(K):
            count = count + jnp.where(lax.shift_right_logical(pos[k].astype(I32), 4) == a_row, 1.0, 0.0)
        last_b = jnp.full((PEER_N_KEYS, tm), -1.0, F32)
        for a in range(K):
            last_b = jnp.where(ra == float(a), count[a:a + 1] - 1.0, last_b)
        e1_ref[h] = jnp.where(ra < float(K), jnp.exp(s1 - v1[0]), 0.0) / zsum
        ks_ref[h] = last_b
        e2_ref[h] = jnp.where(rb < float(K), jnp.exp(s2 - v2[0]), 0.0)
        rb_ref[h] = rb


def _peer_select(pq, keys, tm):
    n = pq.shape[0]
    big = jax.ShapeDtypeStruct((PEER_HEADS, PEER_N_KEYS, n), F32)
    bspec = pl.BlockSpec((PEER_HEADS, PEER_N_KEYS, tm), lambda i: (0, 0, i))
    return pl.pallas_call(
        _peer_select_kernel,
        grid=(n // tm,),
        in_specs=[pl.BlockSpec((tm, pq.shape[1]), lambda i: (i, 0)), pl.BlockSpec(keys.shape, lambda i: (0, 0, 0))],
        out_specs=[bspec] * 4,
        out_shape=[big] * 4,
        compiler_params=_cparams(("parallel",)),
        name="peer_select",
    )(pq, keys)


def _peer_expert_kernel(xn_ref, wd_ref, wu_ref, e1_ref, ks_ref, e2_ref, rb_ref, o_ref, coef_sc, *, krows):
    i = pl.program_id(1)
    n_tiles = pl.num_programs(1) - 1

    @pl.when(i == 0)
    def _():
        o_ref[...] = jnp.zeros(o_ref.shape, F32)
        coef_sc[1] = jnp.zeros(coef_sc.shape[1:], BF16)

    row0 = lax.rem(jnp.minimum(i, n_tiles - 1), SUBLANE // krows) * krows
    gates = []
    for r in range(krows):
        acc = None
        for h in range(PEER_HEADS):
            first = pl.ds(row0 + r, 1)
            term = jnp.where(rb_ref[h] <= ks_ref[h, first, :], e2_ref[h], 0.0) * e1_ref[h, first, :]
            acc = term if acc is None else acc + term
        gates.append(acc)
    gate = jnp.concatenate(gates, axis=0)

    o_ref[...] += _dot_tn(coef_sc[lax.rem(i + 1, 2)], wu_ref[...])
    fold = gate[0:SUBLANE]
    for q in range(1, gate.shape[0] // SUBLANE):
        fold = fold + gate[q * SUBLANE:(q + 1) * SUBLANE]
    fold1 = fold[:, 0:LANE]
    for q in range(1, fold.shape[1] // LANE):
        fold1 = fold1 + fold[:, q * LANE:(q + 1) * LANE]
    bits = pltpu.bitcast(fold1, jnp.uint32)
    zero = lax.shift_right_logical(lax.shift_right_logical(bits, jnp.uint32(16)), jnp.uint32(16))
    o_ref[0:SUBLANE, 0:LANE] += pltpu.bitcast(zero, F32)

    a_t = _dot_nt(wd_ref[...], xn_ref[...])
    gel = 0.5 * a_t * (1.0 + lax.erf(a_t * INV_SQRT2))
    coef_sc[lax.rem(i, 2)] = (gate * gel).astype(BF16)


def _peer_experts(xn, wd, wu, sel, tm, te, n_out):
    n, d = xn.shape
    ne = wd.shape[0]
    krows = te // PEER_N_KEYS
    n_tiles = ne // te
    per_blk = SUBLANE // krows
    e1, ks, e2, rb = sel
    cur = lambda i: jnp.minimum(i, n_tiles - 1)
    once = pl.Buffered(1)
    spec1 = pl.BlockSpec((PEER_HEADS, SUBLANE, tm), lambda j, i: (0, cur(i) // per_blk, j))
    spec2 = pl.BlockSpec((PEER_HEADS, PEER_N_KEYS, tm), lambda j, i: (0, 0, j), pipeline_mode=once)
    return pl.pallas_call(
        functools.partial(_peer_expert_kernel, krows=krows),
        grid=(pl.cdiv(n, tm), n_tiles + 1),
        in_specs=[
            pl.BlockSpec((tm, d), lambda j, i: (j, 0), pipeline_mode=once),
            pl.BlockSpec((te, d), lambda j, i: (cur(i), 0)),
            pl.BlockSpec((te, d), lambda j, i: (jnp.maximum(i - 1, 0), 0)),
            spec1, spec1, spec2, spec2,
        ],
        out_specs=pl.BlockSpec((tm, d), lambda j, i: (j, 0), pipeline_mode=once),
        out_shape=jax.ShapeDtypeStruct((n_out, d), F32),
        scratch_shapes=[pltpu.VMEM((2, te, tm), BF16)],
        compiler_params=_cparams(("parallel", "arbitrary")),
        name="peer_experts",
    )(xn, wd, wu, e1, ks, e2, rb)


def _rope_table(pos):
    inv = ROPE_THETA ** (-jnp.arange(0, ROPE_DIM, 2, dtype=F32) / ROPE_DIM)
    ang = pos.astype(F32)[:, None] * inv[None, :]
    reps = LANE // (ROPE_DIM // 2)
    return jnp.concatenate([jnp.tile(jnp.cos(ang), (1, reps)), jnp.tile(jnp.sin(ang), (1, reps))], axis=1)


def _rot_cols(w):
    half = ROPE_DIM // 2
    return jnp.concatenate([-w[..., half:], w[..., :half]], axis=-1)


def _pad_cols(w, width):
    return jnp.pad(w, [(0, 0)] * (w.ndim - 1) + [(0, width - w.shape[-1])])


def kernel(x_prompt, x_sample, p_prompt, p_sample, cache_kv_latent, cache_k_rope, state_mlstm_C, state_mlstm_n, state_mlstm_m, page_table, g_mix_norm, w_in, g_q_latent, w_uq, g_kv_latent, w_uk, w_uv, b_igate, b_fgate, g_mlstm_head, w_out, g_ffn_norm, peer_w_query, peer_sub_keys, peer_w_down, peer_w_up, g_ple_norm, w_ple, w_ple_gate, g_final_norm):
    B, S, D = x_prompt.shape
    DB, T, _ = x_sample.shape
    depth = w_in.shape[0]
    q_lora, kv_lora = g_q_latent.shape[1], g_kv_latent.shape[1]
    H, DK, DV = MLSTM_HEADS, MLSTM_QK_DIM, MLSTM_V_DIM
    n_p, n_s = B * S, DB * T
    n = n_p + n_s
    n_past = page_table.shape[1] * PAGE_SIZE
    scale = (NOPE_DIM + ROPE_DIM) ** -0.5
    t_pad = SUBLANE * ((T + SUBLANE - 1) // SUBLANE)

    zo, off = {}, 0
    for name, width in (("mq", H * DK), ("mk", H * DK), ("mv", H * DV), ("mo", H * DV), ("cq", q_lora),
                        ("ckv", kv_lora), ("kr", 2 * LANE), ("gate", LANE)):
        zo[name] = off
        off += width
    zw = 1024 * ((off + 1023) // 1024)

    cs = jnp.concatenate([jnp.tile(_rope_table(jnp.arange(S, dtype=I32)), (B, 1)),
                          jnp.tile(_rope_table(n_past + jnp.arange(T, dtype=I32)), (DB, 1))], axis=0)
    tmq = _tile(math.gcd(S, n_s), 128, BF16_SUBLANE)
    chunk_p = math.gcd(S, MLSTM_KERNEL_CHUNK)

    cache_kr_t = jnp.swapaxes(cache_k_rope, 2, 3)
    x = jnp.concatenate([x_prompt.reshape(n_p, D), x_sample.reshape(n_s, D)], axis=0)
    outs = {k: [] for k in ("ckv_p", "kr_p", "ckv_s", "kr_s", "C_p", "n_p", "m_p", "C_s", "n_s", "m_s")}
    for l in range(depth):
        sizes = (q_lora, kv_lora, ROPE_DIM, H * DK, H * DK, H * DV, H * DV, H, H)
        parts, st = [], 0
        for sz in sizes:
            parts.append(w_in[l][:, st:st + sz])
            st += sz
        wcq, wckv, wkr, wmq, wmk, wmv, wmo, wmi, wmf = parts
        w_z = jnp.concatenate([wmq, wmk, wmv, wmo, wcq, wckv, _pad_cols(wkr, LANE), _pad_cols(_rot_cols(wkr), LANE),
                               wmi, wmf], axis=1)
        w_z = _pad_cols(w_z, zw).astype(BF16)
        wq3 = w_uq[l].reshape(q_lora, MLA_HEADS, NOPE_DIM + ROPE_DIM)
        wn = wq3[:, :, :NOPE_DIM].reshape(q_lora, MLA_HEADS * NOPE_DIM).astype(BF16)
        wr = _pad_cols(wq3[:, :, NOPE_DIM:], LANE).reshape(q_lora, MLA_HEADS * LANE).astype(BF16)
        wrr = _pad_cols(_rot_cols(wq3[:, :, NOPE_DIM:]), LANE).reshape(q_lora, MLA_HEADS * LANE).astype(BF16)
        wuk = w_uk[l].reshape(kv_lora, MLA_HEADS, NOPE_DIM).transpose(1, 2, 0).astype(BF16)
        wuv = w_uv[l].reshape(kv_lora, MLA_HEADS, MLA_V_DIM).transpose(1, 0, 2).astype(BF16)
        bias = jnp.stack([b_igate[l], b_fgate[l]]).astype(F32)

        hn = _rmsnorm(x, g_mix_norm[l], BF16)
        z = _matmul(hn, w_z)

        q4, ckv, kr, kvb = _qprep(z, cs, g_q_latent[l], g_kv_latent[l], wn, wr, wrr, wuk, tmq, zo)
        o_mla_p = _attn_prompt(q4, kvb, wuv, B, S, tmq, scale)
        nblk_s = n_s // tmq
        qs = q4[n_p // tmq:].transpose(1, 0, 2, 3).reshape(MLA_HEADS, DB, T, KV_W).transpose(1, 0, 2, 3)
        qs = jnp.pad(qs, ((0, 0), (0, 0), (0, t_pad - T), (0, 0))).reshape(DB, MLA_HEADS * t_pad, KV_W)
        kvn = jnp.pad(kvb[n_p:].reshape(DB, T, KV_W), ((0, 0), (0, t_pad - T), (0, 0)))
        o_mla_s = _attn_sample(page_table, qs, kvn, wuv, cache_kv_latent, cache_kr_t, l, scale, T)
        o_mla = jnp.concatenate([o_mla_p, o_mla_s[:, :T].reshape(n_s, -1).astype(BF16)], axis=0)

        gz = z[:, zo["gate"]:zo["gate"] + 2 * H]
        c0p = jnp.zeros((B, H, DK, DV), F32)
        n0p = jnp.zeros((B, H, DK), F32)
        m0p = jnp.full((B, H), M_EMPTY, F32)
        hg_p, C_p, nn_p, mm_p = _mlstm(z, gz[:n_p].reshape(B, S, 2, H), bias, g_mlstm_head[l], c0p, n0p, m0p,
                                       B, S, chunk_p, chunk_p, MLSTM_HEADS_PER_STEP_PROMPT, zo)
        z_s = jnp.pad(z[n_p:].reshape(DB, T, zw), ((0, 0), (0, t_pad - T), (0, 0))).reshape(DB * t_pad, zw)
        g_s = jnp.pad(gz[n_p:].reshape(DB, T, 2, H), ((0, 0), (0, t_pad - T), (0, 0), (0, 0)))
        hg_s, C_s, nn_s, mm_s = _mlstm(z_s, g_s, bias, g_mlstm_head[l], state_mlstm_C[l].astype(F32),
                                       state_mlstm_n[l].astype(F32), state_mlstm_m[l], DB, t_pad, t_pad, T,
                                       MLSTM_HEADS_PER_STEP_SAMPLE, zo)
        o_ml = jnp.concatenate([hg_p, hg_s.reshape(DB, t_pad, H * DV)[:, :T].reshape(n_s, H * DV)], axis=0)

        x1 = _outproj(o_mla, o_ml, w_out[l].astype(BF16), x)

        tmx = min(PEER_TOKEN_TILE, LANE * pl.cdiv(n, LANE))
        n_pad = tmx * pl.cdiv(n, tmx)
        xn = _rmsnorm(x1, g_ffn_norm[l], BF16, pad_to=n_pad)
        pq = _matmul(xn, peer_w_query[l].astype(BF16))
        keys = peer_sub_keys[l].reshape(2 * PEER_HEADS, PEER_N_KEYS, -1).astype(BF16)
        sel = _peer_select(pq, keys, _tile(n_pad, 256, LANE))
        peer_out = _peer_experts(xn, peer_w_down[l].astype(BF16), peer_w_up[l].astype(BF16), sel, tmx, PEER_EXPERT_TILE, n)

        x2, xg = _add_rmsnorm(x1, peer_out, g_ple_norm[l])
        pcat = jnp.concatenate([p_prompt[l].reshape(n_p, -1), p_sample[l].reshape(n_s, -1)], axis=0).astype(BF16)
        x = _ple(xg, w_ple_gate[l].astype(BF16), pcat, w_ple[l].astype(BF16), x2)

        outs["ckv_p"].append(ckv[:n_p].reshape(B, S, kv_lora))
        outs["kr_p"].append(kr[:n_p].reshape(B, S, ROPE_DIM))
        outs["ckv_s"].append(ckv[n_p:].reshape(DB, T, kv_lora))
        outs["kr_s"].append(kr[n_p:].reshape(DB, T, ROPE_DIM))
        outs["C_p"].append(C_p)
        outs["n_p"].append(nn_p)
        outs["m_p"].append(mm_p)
        outs["C_s"].append(C_s.astype(state_mlstm_C.dtype))
        outs["n_s"].append(nn_s.astype(state_mlstm_n.dtype))
        outs["m_s"].append(mm_s.astype(state_mlstm_m.dtype))

    y_p = _rmsnorm(x, g_final_norm, F32, 0, n_p)
    y_s = _rmsnorm(x, g_final_norm, F32, n_p, n_s)
    st = {k: jnp.stack(v) for k, v in outs.items()}
    return (y_p.reshape(B, S, D), y_s.reshape(DB, T, D), st["ckv_p"], st["kr_p"], st["ckv_s"], st["kr_s"],
            st["C_p"], st["n_p"], st["m_p"], st["C_s"], st["n_s"], st["m_s"])
```
